```python
import math
import jax, jax.numpy as jnp
from jax import lax
import numpy as np

D_MODEL = 1024
BATCH = 32
SEQ = 256
DEPTH = 4
DEC_BATCH = 8
DEC_SEQ = 4096
PAST_LEN = 256

GRID_W = 64
HEAD_DIM = 64
GQA_HEADS = 8
GQA_KV_HEADS = 2
GQA_GROUP = GQA_HEADS // GQA_KV_HEADS
GQA_WIDTH = GQA_HEADS * HEAD_DIM
CONV_WIDTH = D_MODEL // 4
DIFF_HEADS = 4
DIFF_QK_DIM = 32
DIFF_V_DIM = 2 * DIFF_QK_DIM
DIFF_WIDTH = DIFF_HEADS * DIFF_V_DIM
MIX_WIDTH = GQA_WIDTH + CONV_WIDTH + DIFF_WIDTH
IN_SIZES = (GQA_WIDTH, GQA_KV_HEADS * HEAD_DIM, GQA_KV_HEADS * HEAD_DIM,
            CONV_WIDTH, CONV_WIDTH, CONV_WIDTH,
            DIFF_HEADS * 2 * DIFF_QK_DIM, DIFF_HEADS * 2 * DIFF_QK_DIM, DIFF_WIDTH)
IN_WIDTH = sum(IN_SIZES)
D_FF = 2816
CONV_W = 3
Q_BLOCK = 128
ROPE_THETA = 10000.0
NORM_EPS = 1e-6
N_MOD = 6

kernel_name = 'hybrid_prefix_dit_step'


def rms_norm(x, g):
    xf = x.astype(jnp.float32)
    y = xf * lax.rsqrt(jnp.mean(xf * xf, axis=-1, keepdims=True) + NORM_EPS)
    return (y * g.astype(jnp.float32)).astype(x.dtype)


def split_cols(z, sizes):
    out, s = [], 0
    for n in sizes:
        out.append(z[..., s:s + n])
        s += n
    return out


def dwconv3(x, w, b):
    t = x.shape[1]
    xp = jnp.pad(x, ((0, 0), (1, 1), (0, 0)))
    return xp[:, :t] * w[0] + xp[:, 1:t + 1] * w[1] + xp[:, 2:] * w[2] + b


def axial_rope_tables(n_rows, dim):
    row = jnp.repeat(jnp.arange(n_rows), GRID_W).astype(jnp.float32)
    col = jnp.tile(jnp.arange(GRID_W), n_rows).astype(jnp.float32)
    nf = dim // 4
    freqs = ROPE_THETA ** (-jnp.arange(nf, dtype=jnp.float32) / nf)
    ar = row[:, None] * freqs[None, :]
    ac = col[:, None] * freqs[None, :]
    ang = jnp.concatenate([ar, ar, ac, ac], axis=-1)
    return jnp.cos(ang), jnp.sin(ang)


def apply_rope(x, cos, sin):
    d = x.shape[-1]
    xr = x.reshape(*x.shape[:-1], 2, 2, d // 4)
    rot = jnp.stack([-xr[..., 1, :], xr[..., 0, :]], axis=-2).reshape(x.shape)
    bshape = (1, x.shape[1]) + (1,) * (x.ndim - 3) + (d,)
    return (x * cos.reshape(bshape) + rot * sin.reshape(bshape)).astype(x.dtype)


def over_query_blocks(fn, q):
    b, t = q.shape[:2]
    nb = t // Q_BLOCK
    qb = jnp.moveaxis(q.reshape(b, nb, Q_BLOCK, *q.shape[2:]), 1, 0)
    out = lax.map(fn, qb)
    return jnp.moveaxis(out, 0, 1).reshape(b, t, *out.shape[3:])


def gqa_attend(q, k, v):
    b, t = q.shape[:2]
    q5 = q.reshape(b, t, GQA_KV_HEADS, GQA_GROUP, HEAD_DIM)
    scale = HEAD_DIM ** -0.5

    def blk(qb):
        s = jnp.einsum('bqkgd,bskd->bkgqs', qb, k).astype(jnp.float32) * scale
        p = jax.nn.softmax(s, axis=-1).astype(v.dtype)
        return jnp.einsum('bkgqs,bskd->bqkgd', p, v)

    return over_query_blocks(blk, q5).reshape(b, t, GQA_WIDTH)


def diff_attend(q, k, v, lam):
    scale = DIFF_QK_DIM ** -0.5

    def blk(qb):
        s = jnp.einsum('bqhcd,bshcd->bhcqs', qb, k).astype(jnp.float32) * scale
        p = jax.nn.softmax(s, axis=-1)
        a = (p[:, :, 0] - lam * p[:, :, 1]).astype(v.dtype)
        return jnp.einsum('bhqs,bshd->bqhd', a, v)

    return over_query_blocks(blk, q)


def trunk_layer(x, cond, lam_init, rope, ctx, w_mod, b_mod, norm1_g, w_in, gqa_qn_g, gqa_kn_g,
                conv_w, conv_b, diff_qn_g, diff_kn_g, diff_lambda, diff_subln_g, w_out,
                norm2_g, ffn_up, ffn_conv_w, ffn_conv_b, ffn_down):
    b, t = x.shape[:2]
    mods = (jax.nn.silu(cond) @ w_mod + b_mod).reshape(cond.shape[0], 1, N_MOD, D_MODEL)
    sh1, sc1, g1, sh2, sc2, g2 = [mods[:, :, i] for i in range(N_MOD)]

    h = rms_norm(x, norm1_g) * (1 + sc1) + sh1
    z = h @ w_in
    qg, kg, vg, cb, cc, cu, qd, kd, vd = split_cols(z, IN_SIZES)
    qg = rms_norm(qg.reshape(b, t, GQA_HEADS, HEAD_DIM), gqa_qn_g)
    kg = rms_norm(kg.reshape(b, t, GQA_KV_HEADS, HEAD_DIM), gqa_kn_g)
    vg = vg.reshape(b, t, GQA_KV_HEADS, HEAD_DIM)
    qd = rms_norm(qd.reshape(b, t, DIFF_HEADS, 2, DIFF_QK_DIM), diff_qn_g)
    kd = rms_norm(kd.reshape(b, t, DIFF_HEADS, 2, DIFF_QK_DIM), diff_kn_g)
    vd = vd.reshape(b, t, DIFF_HEADS, DIFF_V_DIM)
    own = (kg, vg, kd, vd)

    if rope is None:
        kg_all, vg_all, kd_all, vd_all = kg, vg, kd, vd
    else:
        cos64, sin64, cos32, sin32 = rope
        qg = apply_rope(qg, cos64, sin64)
        qd = apply_rope(qd, cos32, sin32)
        ck_g, cv_g, ck_d, cv_d = ctx
        kg_all = jnp.concatenate([apply_rope(kg, cos64, sin64), ck_g.astype(kg.dtype)], axis=1)
        vg_all = jnp.concatenate([vg, cv_g.astype(vg.dtype)], axis=1)
        kd_all = jnp.concatenate([apply_rope(kd, cos32, sin32), ck_d.astype(kd.dtype)], axis=1)
        vd_all = jnp.concatenate([vd, cv_d.astype(vd.dtype)], axis=1)

    o_g = gqa_attend(qg, kg_all, vg_all)
    o_c = cb * dwconv3(cc * cu, conv_w, conv_b)
    lf = diff_lambda.astype(jnp.float32)
    lam = jnp.exp(jnp.sum(lf[0] * lf[1])) - jnp.exp(jnp.sum(lf[2] * lf[3])) + lam_init
    o_d = diff_attend(qd, kd_all, vd_all, lam)
    o_d = (rms_norm(o_d, diff_subln_g) * (1.0 - lam_init)).reshape(b, t, DIFF_WIDTH)
    x = x + g1 * (jnp.concatenate([o_g, o_c, o_d], axis=-1) @ w_out)

    h = rms_norm(x, norm2_g) * (1 + sc2) + sh2
    a, u = jnp.split(h @ ffn_up, 2, axis=-1)
    f = jax.nn.silu(dwconv3(a, ffn_conv_w, ffn_conv_b)) * u
    x = x + g2 * (f @ ffn_down)
    return x, own


def setup_inputs(seed: int = 0) -> dict:
    key = jax.random.key(seed)
    ks = jax.random.split(key, 32)
    f32 = jnp.float32
    nrm = lambda k, shape, s: jax.random.normal(k, shape, f32) * s
    return {
        'x_prompt': nrm(ks[0], (BATCH, SEQ, D_MODEL), 1.0),
        'x_sample': nrm(ks[1], (DEC_BATCH, DEC_SEQ, D_MODEL), 1.0),
        'cache_gqa_k': nrm(ks[2], (DEC_BATCH, DEPTH, PAST_LEN, GQA_KV_HEADS, HEAD_DIM), 1.0),
        'cache_gqa_v': nrm(ks[3], (DEC_BATCH, DEPTH, PAST_LEN, GQA_KV_HEADS, HEAD_DIM), 1.0),
        'cache_diff_k': nrm(ks[4], (DEC_BATCH, DEPTH, PAST_LEN, DIFF_HEADS, 2, DIFF_QK_DIM), 1.0),
        'cache_diff_v': nrm(ks[5], (DEC_BATCH, DEPTH, PAST_LEN, DIFF_HEADS, DIFF_V_DIM), 1.0),
        'c': nrm(ks[6], (DEC_BATCH, D_MODEL), 1.0),
        'c_ctx': nrm(ks[7], (D_MODEL,), 1.0),
        'w_mod': nrm(ks[8], (DEPTH, D_MODEL, N_MOD * D_MODEL), 0.5 * D_MODEL ** -0.5),
        'b_mod': nrm(ks[9], (DEPTH, N_MOD * D_MODEL), 0.01),
        'norm1_g': 1.0 + nrm(ks[10], (DEPTH, D_MODEL), 0.02),
        'w_in': nrm(ks[11], (DEPTH, D_MODEL, IN_WIDTH), D_MODEL ** -0.5),
        'gqa_qn_g': 1.0 + nrm(ks[12], (DEPTH, HEAD_DIM), 0.02),
        'gqa_kn_g': 1.0 + nrm(ks[13], (DEPTH, HEAD_DIM), 0.02),
        'conv_w': nrm(ks[14], (DEPTH, CONV_W, CONV_WIDTH), CONV_W ** -0.5),
        'conv_b': nrm(ks[15], (DEPTH, CONV_WIDTH), 0.01),
        'diff_qn_g': 1.0 + nrm(ks[16], (DEPTH, DIFF_QK_DIM), 0.02),
        'diff_kn_g': 1.0 + nrm(ks[17], (DEPTH, DIFF_QK_DIM), 0.02),
        'diff_lambda': nrm(ks[18], (DEPTH, 4, DIFF_QK_DIM), 0.1),
        'diff_subln_g': 1.0 + nrm(ks[19], (DEPTH, DIFF_V_DIM), 0.02),
        'w_out': nrm(ks[20], (DEPTH, MIX_WIDTH, D_MODEL), MIX_WIDTH ** -0.5),
        'norm2_g': 1.0 + nrm(ks[21], (DEPTH, D_MODEL), 0.02),
        'ffn_up': nrm(ks[22], (DEPTH, D_MODEL, 2 * D_FF), D_MODEL ** -0.5),
        'ffn_conv_w': nrm(ks[23], (DEPTH, CONV_W, D_FF), CONV_W ** -0.5),
        'ffn_conv_b': nrm(ks[24], (DEPTH, D_FF), 0.01),
        'ffn_down': nrm(ks[25], (DEPTH, D_FF, D_MODEL), D_FF ** -0.5),
    }


def reference(x_prompt, x_sample, cache_gqa_k, cache_gqa_v, cache_diff_k, cache_diff_v, c, c_ctx,
              w_mod, b_mod, norm1_g, w_in, gqa_qn_g, gqa_kn_g, conv_w, conv_b, diff_qn_g,
              diff_kn_g, diff_lambda, diff_subln_g, w_out, norm2_g, ffn_up, ffn_conv_w,
              ffn_conv_b, ffn_down):
    def layer_weights(l):
        return (w_mod[l], b_mod[l], norm1_g[l], w_in[l], gqa_qn_g[l], gqa_kn_g[l], conv_w[l],
                conv_b[l], diff_qn_g[l], diff_kn_g[l], diff_lambda[l], diff_subln_g[l], w_out[l],
                norm2_g[l], ffn_up[l], ffn_conv_w[l], ffn_conv_b[l], ffn_down[l])

    xp = x_prompt
    cond_ctx = c_ctx[None, :]
    ks_g, vs_g, ks_d, vs_d = [], [], [], []
    for l in range(DEPTH):
        lam_init = 0.8 - 0.6 * math.exp(-0.3 * l)
        xp, (kg, vg, kd, vd) = trunk_layer(xp, cond_ctx, lam_init, None, None, *layer_weights(l))
        ks_g.append(kg)
        vs_g.append(vg)
        ks_d.append(kd)
        vs_d.append(vd)
    new_gqa_k = jnp.stack(ks_g, axis=1)
    new_gqa_v = jnp.stack(vs_g, axis=1)
    new_diff_k = jnp.stack(ks_d, axis=1)
    new_diff_v = jnp.stack(vs_d, axis=1)

    n_rows = x_sample.shape[1] // GRID_W
    cos64, sin64 = axial_rope_tables(n_rows, HEAD_DIM)
    cos32, sin32 = axial_rope_tables(n_rows, DIFF_QK_DIM)
    rope = (cos64, sin64, cos32, sin32)
    xs = x_sample
    for l in range(DEPTH):
        lam_init = 0.8 - 0.6 * math.exp(-0.3 * l)
        ctx = (cache_gqa_k[:, l], cache_gqa_v[:, l], cache_diff_k[:, l], cache_diff_v[:, l])
        xs, _ = trunk_layer(xs, c, lam_init, rope, ctx, *layer_weights(l))

    return (xp, xs, new_gqa_k, new_gqa_v, new_diff_k, new_diff_v)
```

```python
import functools
import math

import jax
import jax.numpy as jnp
from jax import lax
from jax.experimental import pallas as pl
from jax.experimental.pallas import tpu as pltpu

F32 = jnp.float32
BF16 = jnp.bfloat16

D_MODEL = 1024
GRID_W = 64
HEAD_DIM = 64
GQA_HEADS = 8
GQA_KV_HEADS = 2
GQA_WIDTH = GQA_HEADS * HEAD_DIM
CONV_WIDTH = 256
DIFF_HEADS = 4
DIFF_QK_DIM = 32
DIFF_V_DIM = 64
DIFF_WIDTH = DIFF_HEADS * DIFF_V_DIM
IN_WIDTH = 2304
D_FF = 2816
ROPE_THETA = 10000.0
NORM_EPS = 1e-6
N_MOD = 6
LANES = 128
MOD_ROWS = 16

OFF_QG, OFF_KG, OFF_VG = 0, 512, 640
OFF_CB, OFF_CC, OFF_CU = 768, 1024, 1280
OFF_QD, OFF_KD, OFF_VD = 1536, 1792, 2048

HALO = 16
PAD = 8

VMEM_LIMIT = 60 * 1024 * 1024


def _cparams():
    return pltpu.CompilerParams(vmem_limit_bytes=VMEM_LIMIT)


def _const_spec(shape):
    nd = len(shape)
    return pl.BlockSpec(shape, lambda *_: (0,) * nd, pipeline_mode=pl.Buffered(1))


def _mods_kernel(cond_ref, w_ref, b_ref, o_ref):
    cnd = cond_ref[...]
    act = cnd * jax.nn.sigmoid(cnd)
    o_ref[0] = jnp.dot(act, w_ref[0], preferred_element_type=F32,
                       precision=lax.Precision.HIGHEST) + b_ref[0]


def _mods_call(cond_all, w_mod, b_mod):
    depth = w_mod.shape[0]
    width = N_MOD * D_MODEL
    tn = 1024
    out = pl.pallas_call(
        _mods_kernel,
        out_shape=jax.ShapeDtypeStruct((depth, MOD_ROWS, width), F32),
        grid=(depth, width // tn),
        in_specs=[
            pl.BlockSpec((MOD_ROWS, D_MODEL), lambda l, j: (0, 0)),
            pl.BlockSpec((1, D_MODEL, tn), lambda l, j: (l, 0, j)),
            pl.BlockSpec((1, 1, tn), lambda l, j: (l, 0, j)),
        ],
        out_specs=pl.BlockSpec((1, MOD_ROWS, tn), lambda l, j: (l, 0, j)),
        compiler_params=_cparams(),
        name="adaln_mods",
    )(cond_all, w_mod, b_mod.reshape(depth, 1, width))
    return out.reshape(depth, MOD_ROWS, N_MOD, D_MODEL)


def _dup_halves(x128):
    lo = lax.broadcasted_iota(jnp.int32, x128.shape, 1) < HEAD_DIM
    swapped = pltpu.roll(x128, HEAD_DIM, 1)
    return jnp.where(lo, x128, swapped), jnp.where(lo, swapped, x128)


def _cache_prep_kernel(gk_ref, gv_ref, dk_ref, dv_ref, kk_ref, vog_ref, kd_ref, vod_ref):
    ones = jnp.ones((gk_ref.shape[2], LANES), BF16)
    k0, k1 = _dup_halves(gk_ref[0, 0])
    kk_ref[0, 0, 0] = k0.astype(BF16)
    kk_ref[0, 0, 1] = k1.astype(BF16)
    v0, v1 = _dup_halves(gv_ref[0, 0])
    vog_ref[0, 0, 0, :, :LANES] = v0.astype(BF16)
    vog_ref[0, 0, 1, :, :LANES] = v1.astype(BF16)
    vog_ref[0, 0, 0, :, LANES:] = ones
    vog_ref[0, 0, 1, :, LANES:] = ones
    kd_ref[0, 0] = dk_ref[0, 0].astype(BF16)
    dv = dv_ref[0, 0]
    vod_ref[0, 0, 0, :, :LANES] = dv[:, :LANES].astype(BF16)
    vod_ref[0, 0, 1, :, :LANES] = dv[:, LANES:].astype(BF16)
    vod_ref[0, 0, 0, :, LANES:] = ones
    vod_ref[0, 0, 1, :, LANES:] = ones


def _cache_prep_call(cache_gqa_k, cache_gqa_v, cache_diff_k, cache_diff_v):
    b, depth, s = cache_gqa_k.shape[:3]
    gk = cache_gqa_k.reshape(b, depth, s, LANES)
    gv = cache_gqa_v.reshape(b, depth, s, LANES)
    dk = cache_diff_k.reshape(b, depth, s, 2 * LANES)
    dv = cache_diff_v.reshape(b, depth, s, 2 * LANES)
    idx4 = lambda i, l: (i, l, 0, 0)
    idx5 = lambda i, l: (i, l, 0, 0, 0)
    return pl.pallas_call(
        _cache_prep_kernel,
        out_shape=(
            jax.ShapeDtypeStruct((b, depth, 2, s, LANES), BF16),
            jax.ShapeDtypeStruct((b, depth, 2, s, 2 * LANES), BF16),
            jax.ShapeDtypeStruct((b, depth, s, 2 * LANES), BF16),
            jax.ShapeDtypeStruct((b, depth, 2, s, 2 * LANES), BF16),
        ),
        grid=(b, depth),
        in_specs=[
            pl.BlockSpec((1, 1, s, LANES), idx4),
            pl.BlockSpec((1, 1, s, LANES), idx4),
            pl.BlockSpec((1, 1, s, 2 * LANES), idx4),
            pl.BlockSpec((1, 1, s, 2 * LANES), idx4),
        ],
        out_specs=(
            pl.BlockSpec((1, 1, 2, s, LANES), idx5),
            pl.BlockSpec((1, 1, 2, s, 2 * LANES), idx5),
            pl.BlockSpec((1, 1, s, 2 * LANES), idx4),
            pl.BlockSpec((1, 1, 2, s, 2 * LANES), idx5),
        ),
        compiler_params=_cparams(),
        name="cache_prep",
    )(gk, gv, dk, dv)


def _group_inv_rms(z, g_ref, group):
    sq = z * z
    hi = sq.astype(BF16)
    lo = (sq - hi.astype(F32)).astype(BF16)
    width = z.shape[1]
    gmat = g_ref[:width, :width]
    ssum = (jnp.dot(hi, gmat, preferred_element_type=F32)
            + jnp.dot(lo, gmat, preferred_element_type=F32))
    return lax.rsqrt(ssum * (1.0 / group) + NORM_EPS)


def _rope_chunk(y, cos, sin_up, sin_dn, quarter):
    up = pltpu.roll(y, LANES - quarter, 1)
    dn = pltpu.roll(y, quarter, 1)
    return y * cos + up * sin_up + dn * sin_dn


def _pre_kernel(*refs, rope, own):
    it = iter(refs)
    x_ref, mod_ref, n1g_ref, w_in_ref = next(it), next(it), next(it), next(it)
    gq_ref, gk_ref, gqd_ref, gkd_ref = next(it), next(it), next(it), next(it)
    g64_ref, g32_ref = next(it), next(it)
    if rope:
        c64, su64, sd64, c32, su32, sd32 = (next(it) for _ in range(6))
    qg_ref, kk_ref, vog_ref, qd_ref, kd_ref, vod_ref, cb_ref, ccu_ref = (next(it) for _ in range(8))
    if own:
        kown_ref, vown_ref, kdown_ref, vdown_ref = (next(it) for _ in range(4))

    x = x_ref[0]
    mods = mod_ref[0]
    sh1, sc1 = mods[0:1], mods[1:2]
    inv = lax.rsqrt(jnp.mean(x * x, axis=-1, keepdims=True) + NORM_EPS)
    h = (x * inv) * n1g_ref[...] * (1.0 + sc1) + sh1
    z = jnp.dot(h.astype(BF16), w_in_ref[...], preferred_element_type=F32)
    tm = x.shape[0]

    def rope64(y):
        if not rope:
            return y
        return _rope_chunk(y, c64[...], su64[...], sd64[...], HEAD_DIM // 4)

    def rope32(y):
        if not rope:
            return y
        return _rope_chunk(y, c32[...], su32[...], sd32[...], DIFF_QK_DIM // 4)

    for c in range(2):
        zc = z[:, OFF_QG + 256 * c: OFF_QG + 256 * (c + 1)]
        y = zc * _group_inv_rms(zc, g64_ref, HEAD_DIM)
        for j in range(2):
            yj = rope64(y[:, LANES * j: LANES * (j + 1)] * gq_ref[...])
            qg_ref[0, :, 256 * c + LANES * j: 256 * c + LANES * (j + 1)] = yj.astype(BF16)

    zk = z[:, OFF_KG: OFF_KG + LANES]
    zv = z[:, OFF_VG: OFF_VG + LANES]
    kn = zk * _group_inv_rms(zk, g64_ref, HEAD_DIM) * gk_ref[...]
    if own:
        kown_ref[0] = kn
        vown_ref[0] = zv
    k0, k1 = _dup_halves(rope64(kn))
    kk_ref[0, 0] = k0.astype(BF16)
    kk_ref[0, 1] = k1.astype(BF16)
    v0, v1 = _dup_halves(zv)
    ones = jnp.ones((tm, LANES), BF16)
    vog_ref[0, 0, :, :LANES] = v0.astype(BF16)
    vog_ref[0, 1, :, :LANES] = v1.astype(BF16)
    vog_ref[0, 0, :, LANES:] = ones
    vog_ref[0, 1, :, LANES:] = ones

    cb_ref[0] = z[:, OFF_CB: OFF_CB + CONV_WIDTH]
    ccu_ref[0] = z[:, OFF_CC: OFF_CC + CONV_WIDTH] * z[:, OFF_CU: OFF_CU + CONV_WIDTH]

    zq = z[:, OFF_QD: OFF_QD + 256]
    yq = zq * _group_inv_rms(zq, g32_ref, DIFF_QK_DIM)
    zkd = z[:, OFF_KD: OFF_KD + 256]
    ykd = zkd * _group_inv_rms(zkd, g32_ref, DIFF_QK_DIM)
    zvd = z[:, OFF_VD: OFF_VD + 256]
    for j in range(2):
        sl = slice(LANES * j, LANES * (j + 1))
        qd_ref[0, :, sl] = rope32(yq[:, sl] * gqd_ref[...]).astype(BF16)
        kdj = ykd[:, sl] * gkd_ref[...]
        if own:
            kdown_ref[0, :, sl] = kdj
        kd_ref[0, :, sl] = rope32(kdj).astype(BF16)
        vod_ref[0, j, :, :LANES] = zvd[:, sl].astype(BF16)
        vod_ref[0, j, :, LANES:] = ones
    if own:
        vdown_ref[0] = zvd


def _pre_call(x, mods_l, mod_row, n1g, w_in, gq, gk, gqd, gkd, g64, g32, rope_tabs, own, tm):
    b, t, d = x.shape
    nt = t // tm
    rope = rope_tabs is not None
    in_specs = [
        pl.BlockSpec((1, tm, d), lambda i, j: (i, j, 0)),
        pl.BlockSpec((1, N_MOD, d), lambda i, j: (mod_row(i), 0, 0)),
        _const_spec((1, d)),
        _const_spec((d, IN_WIDTH)),
        _const_spec((1, LANES)), _const_spec((1, LANES)), _const_spec((1, LANES)), _const_spec((1, LANES)),
        _const_spec((256, 256)), _const_spec((256, 256)),
    ]
    args = [x, mods_l, n1g, w_in, gq, gk, gqd, gkd, g64, g32]
    if rope:
        in_specs += [pl.BlockSpec((tm, LANES), lambda i, j: (j, 0))] * 6
        args += list(rope_tabs)
    tok = lambda w, dt: jax.ShapeDtypeStruct((b, t, w), dt)
    tok_spec = lambda w: pl.BlockSpec((1, tm, w), lambda i, j: (i, j, 0))
    pair = lambda w: jax.ShapeDtypeStruct((b, 2, t, w), BF16)
    pair_spec = lambda w: pl.BlockSpec((1, 2, tm, w), lambda i, j: (i, 0, j, 0))
    out_shape = [tok(GQA_WIDTH, BF16), pair(LANES), pair(2 * LANES), tok(256, BF16), tok(256, BF16),
                 pair(2 * LANES), tok(CONV_WIDTH, F32), tok(CONV_WIDTH, F32)]
    out_specs = [tok_spec(GQA_WIDTH), pair_spec(LANES), pair_spec(2 * LANES), tok_spec(256), tok_spec(256),
                 pair_spec(2 * LANES), tok_spec(CONV_WIDTH), tok_spec(CONV_WIDTH)]
    if own:
        out_shape += [tok(LANES, F32), tok(LANES, F32), tok(256, F32), tok(256, F32)]
        out_specs += [tok_spec(LANES), tok_spec(LANES), tok_spec(256), tok_spec(256)]
    return pl.pallas_call(
        functools.partial(_pre_kernel, rope=rope, own=own),
        out_shape=tuple(out_shape),
        grid=(b, nt),
        in_specs=in_specs,
        out_specs=tuple(out_specs),
        compiler_params=_cparams(),
        name="pre_attention",
    )(*args)


def _softmax_pv(lhs, keys, vals):
    scores = [lax.dot_general(lhs, k, (((1,), (1,)), ((), ())), preferred_element_type=F32) for k in keys]
    m = functools.reduce(jnp.maximum, [jnp.max(s, axis=-1, keepdims=True) for s in scores])
    acc = None
    for s, v in zip(scores, vals):
        p = jnp.exp(s - m).astype(BF16)
        part = jnp.dot(p, v, preferred_element_type=F32)
        acc = part if acc is None else acc + part
    return acc[:, :LANES] / acc[:, LANES:]


def _gqa_kernel(*refs, has_cache):
    if has_cache:
        q_ref, kk_ref, vo_ref, kkc_ref, voc_ref, o_ref = refs
        keys = [kk_ref[0, 0], kkc_ref[0, 0, 0]]
        vals = [vo_ref[0, 0], voc_ref[0, 0, 0]]
    else:
        q_ref, kk_ref, vo_ref, o_ref = refs
        keys = [kk_ref[0, 0]]
        vals = [vo_ref[0, 0]]
    q = q_ref[0]
    tq = q.shape[0]
    lo = lax.broadcasted_iota(jnp.int32, (tq, LANES), 1) < HEAD_DIM
    zero = jnp.zeros((tq, LANES), BF16)
    qa, qb = q[:, :LANES], q[:, LANES:]
    lhs = jnp.concatenate([jnp.where(lo, qa, zero), jnp.where(lo, zero, qa),
                           jnp.where(lo, qb, zero), jnp.where(lo, zero, qb)], axis=0)
    r = _softmax_pv(lhs, keys, vals)
    oa = jnp.where(lo, r[0:tq], r[tq:2 * tq])
    ob = jnp.where(lo, r[2 * tq:3 * tq], r[3 * tq:4 * tq])
    o_ref[0, :, :LANES] = oa.astype(BF16)
    o_ref[0, :, LANES:] = ob.astype(BF16)


def _gqa_call(qg, kk, vog, cache, layer, tq):
    b, t, _ = qg.shape
    nq = t // tq
    in_specs = [
        pl.BlockSpec((1, tq, 256), lambda i, h, j: (i, j, h)),
        pl.BlockSpec((1, 1, t, LANES), lambda i, h, j: (i, h, 0, 0)),
        pl.BlockSpec((1, 1, t, 2 * LANES), lambda i, h, j: (i, h, 0, 0)),
    ]
    args = [qg, kk, vog]
    if cache is not None:
        kkc, voc = cache
        s = kkc.shape[3]
        in_specs += [
            pl.BlockSpec((1, 1, 1, s, LANES), lambda i, h, j: (i, layer, h, 0, 0)),
            pl.BlockSpec((1, 1, 1, s, 2 * LANES), lambda i, h, j: (i, layer, h, 0, 0)),
        ]
        args += [kkc, voc]
    return pl.pallas_call(
        functools.partial(_gqa_kernel, has_cache=cache is not None),
        out_shape=jax.ShapeDtypeStruct((b, t, GQA_WIDTH), BF16),
        grid=(b, GQA_KV_HEADS, nq),
        in_specs=in_specs,
        out_specs=pl.BlockSpec((1, tq, 256), lambda i, h, j: (i, j, h)),
        compiler_params=_cparams(),
        name="gqa_attention",
    )(*args)


def _diff_kernel(*refs, has_cache, lam_init):
    if has_cache:
        q_ref, k_ref, vo_ref, kc_ref, voc_ref, lam_ref, sg_ref, o_ref = refs
        keys = [k_ref[0], kc_ref[0, 0]]
        vals = [vo_ref[0, 0], voc_ref[0, 0, 0]]
    else:
        q_ref, k_ref, vo_ref, lam_ref, sg_ref, o_ref = refs
        keys = [k_ref[0]]
        vals = [vo_ref[0, 0]]
    q = q_ref[0]
    tq = q.shape[0]
    lane = lax.broadcasted_iota(jnp.int32, (tq, LANES), 1)
    zero = jnp.zeros((tq, LANES), BF16)
    lhs = jnp.concatenate(
        [jnp.where((lane >= DIFF_QK_DIM * u) & (lane < DIFF_QK_DIM * (u + 1)), q, zero) for u in range(4)],
        axis=0)
    r = _softmax_pv(lhs, keys, vals)
    lf = lam_ref[...]
    lam = (jnp.exp(jnp.sum(lf[0:1] * lf[1:2], axis=-1, keepdims=True))
           - jnp.exp(jnp.sum(lf[2:3] * lf[3:4], axis=-1, keepdims=True)) + lam_init)
    lo = lane < DIFF_V_DIM
    o = jnp.where(lo, r[0:tq] - lam * r[tq:2 * tq], r[2 * tq:3 * tq] - lam * r[3 * tq:4 * tq])
    sq = o * o
    ms_lo = jnp.sum(jnp.where(lo, sq, 0.0), axis=-1, keepdims=True) * (1.0 / DIFF_V_DIM)
    ms_hi = jnp.sum(jnp.where(lo, 0.0, sq), axis=-1, keepdims=True) * (1.0 / DIFF_V_DIM)
    inv = jnp.where(lo, lax.rsqrt(ms_lo + NORM_EPS), lax.rsqrt(ms_hi + NORM_EPS))
    o_ref[0] = ((o * inv) * sg_ref[...] * (1.0 - lam_init)).astype(BF16)


def _diff_call(qd, kd, vod, cache, layer, lam_p, subg, lam_init, tq):
    b, t, _ = qd.shape
    nq = t // tq
    in_specs = [
        pl.BlockSpec((1, tq, LANES), lambda i, h, j: (i, j, h)),
        pl.BlockSpec((1, t, LANES), lambda i, h, j: (i, 0, h)),
        pl.BlockSpec((1, 1, t, 2 * LANES), lambda i, h, j: (i, h, 0, 0)),
    ]
    args = [qd, kd, vod]
    if cache is not None:
        kdc, vodc = cache
        s = kdc.shape[2]
        in_specs += [
            pl.BlockSpec((1, 1, s, LANES), lambda i, h, j: (i, layer, 0, h)),
            pl.BlockSpec((1, 1, 1, s, 2 * LANES), lambda i, h, j: (i, layer, h, 0, 0)),
        ]
        args += [kdc, vodc]
    in_specs += [pl.BlockSpec((4, DIFF_QK_DIM), lambda i, h, j: (0, 0)),
                 pl.BlockSpec((1, LANES), lambda i, h, j: (0, 0))]
    args += [lam_p, subg]
    return pl.pallas_call(
        functools.partial(_diff_kernel, has_cache=cache is not None, lam_init=lam_init),
        out_shape=jax.ShapeDtypeStruct((b, t, DIFF_WIDTH), BF16),
        grid=(b, 2, nq),
        in_specs=in_specs,
        out_specs=pl.BlockSpec((1, tq, LANES), lambda i, h, j: (i, j, h)),
        compiler_params=_cparams(),
        name="diff_attention",
    )(*args)


def _post_kernel(*refs, halo, tm, t_len):
    it = iter(refs)
    n_blk = 3 if halo else 1

    def take():
        blocks = [next(it) for _ in range(n_blk)]
        if halo:
            return jnp.concatenate([blocks[0][0], blocks[1][0], blocks[2][0]], axis=0)
        return blocks[0][0]

    xe, og, od, cb, ccu = take(), take(), take(), take(), take()
    mod_ref, cw_ref, cbias_ref, wout_ref, n2g_ref = (next(it) for _ in range(5))
    up_ref, fcw_ref, fcb_ref, down_ref = (next(it) for _ in range(4))
    out_ref, ccu_scr, a_scr = next(it), next(it), next(it)

    h0 = HALO if halo else 0
    rows = tm + 2 * h0
    mods = mod_ref[0]
    g1, sh2, sc2, g2 = mods[2:3], mods[3:4], mods[4:5], mods[5:6]

    if halo:
        pos = pl.program_id(1) * tm - h0 + lax.broadcasted_iota(jnp.int32, (rows, 1), 0)
        valid = (pos >= 0) & (pos < t_len)
        ccu = jnp.where(valid, ccu, 0.0)

    ccu_scr[0:PAD] = jnp.zeros((PAD, CONV_WIDTH), F32)
    ccu_scr[PAD + rows: 2 * PAD + rows] = jnp.zeros((PAD, CONV_WIDTH), F32)
    ccu_scr[PAD: PAD + rows] = ccu
    cw = cw_ref[...]
    conv = (ccu_scr[PAD - 1: PAD - 1 + rows] * cw[0:1] + ccu * cw[1:2]
            + ccu_scr[PAD + 1: PAD + 1 + rows] * cw[2:3] + cbias_ref[...])
    oc = (cb * conv).astype(BF16)

    y = (jnp.dot(og, wout_ref[0:GQA_WIDTH], preferred_element_type=F32)
         + jnp.dot(oc, wout_ref[GQA_WIDTH: GQA_WIDTH + CONV_WIDTH], preferred_element_type=F32)
         + jnp.dot(od, wout_ref[GQA_WIDTH + CONV_WIDTH:], preferred_element_type=F32))
    xmid = xe + g1 * y

    inv = lax.rsqrt(jnp.mean(xmid * xmid, axis=-1, keepdims=True) + NORM_EPS)
    h2 = ((xmid * inv) * n2g_ref[...] * (1.0 + sc2) + sh2).astype(BF16)
    a = jnp.dot(h2, up_ref[:, :D_FF], preferred_element_type=F32)
    if halo:
        a = jnp.where(valid, a, 0.0)
    a_scr[0:PAD] = jnp.zeros((PAD, D_FF), F32)
    a_scr[PAD + rows: 2 * PAD + rows] = jnp.zeros((PAD, D_FF), F32)
    a_scr[PAD: PAD + rows] = a
    base = PAD + h0
    fcw = fcw_ref[...]
    ac = (a_scr[base - 1: base - 1 + tm] * fcw[0:1] + a_scr[base: base + tm] * fcw[1:2]
          + a_scr[base + 1: base + 1 + tm] * fcw[2:3] + fcb_ref[...])
    u = jnp.dot(h2[h0: h0 + tm], up_ref[:, D_FF:], preferred_element_type=F32)
    f = (ac * jax.nn.sigmoid(ac) * u).astype(BF16)
    out_ref[0] = xmid[h0: h0 + tm] + g2 * jnp.dot(f, down_ref[...], preferred_element_type=F32)


def _post_call(x, og, od, cb, ccu, mods_l, mod_row, cw, cbias, w_out, n2g, up, fcw, fcb, down, tm):
    b, t, d = x.shape
    nt = t // tm
    halo = nt > 1
    per = tm // HALO
    last = t // HALO - 1

    def specs(width):
        main = pl.BlockSpec((1, tm, width), lambda i, j: (i, j, 0))
        if not halo:
            return [main]
        prev = pl.BlockSpec((1, HALO, width), lambda i, j: (i, jnp.maximum(j * per - 1, 0), 0))
        nxt = pl.BlockSpec((1, HALO, width), lambda i, j: (i, jnp.minimum((j + 1) * per, last), 0))
        return [prev, main, nxt]

    in_specs, args = [], []
    for arr in (x, og, od, cb, ccu):
        sp = specs(arr.shape[-1])
        in_specs += sp
        args += [arr] * len(sp)
    in_specs += [
        pl.BlockSpec((1, N_MOD, d), lambda i, j: (mod_row(i), 0, 0)),
        _const_spec((3, CONV_WIDTH)), _const_spec((1, CONV_WIDTH)),
        _const_spec((d, d)), _const_spec((1, d)),
        _const_spec((d, 2 * D_FF)), _const_spec((3, D_FF)), _const_spec((1, D_FF)),
        _const_spec((D_FF, d)),
    ]
    args += [mods_l, cw, cbias, w_out, n2g, up, fcw, fcb, down]
    rows = tm + (2 * HALO if halo else 0)
    return pl.pallas_call(
        functools.partial(_post_kernel, halo=halo, tm=tm, t_len=t),
        out_shape=jax.ShapeDtypeStruct((b, t, d), F32),
        grid=(b, nt),
        in_specs=in_specs,
        out_specs=pl.BlockSpec((1, tm, d), lambda i, j: (i, j, 0)),
        scratch_shapes=[pltpu.VMEM((rows + 2 * PAD, CONV_WIDTH), F32),
                        pltpu.VMEM((rows + 2 * PAD, D_FF), F32)],
        compiler_params=_cparams(),
        name="post_attention",
    )(*args)


def _rope_tables(n_rows, dim):
    row = jnp.repeat(jnp.arange(n_rows), GRID_W).astype(F32)
    col = jnp.tile(jnp.arange(GRID_W), n_rows).astype(F32)
    nf = dim // 4
    freqs = ROPE_THETA ** (-jnp.arange(nf, dtype=F32) / nf)
    ar = row[:, None] * freqs[None, :]
    ac = col[:, None] * freqs[None, :]
    ang = jnp.concatenate([ar, ar, ac, ac], axis=-1)
    cos, sin = jnp.cos(ang), jnp.sin(ang)
    first = ((jnp.arange(dim) // nf) % 2 == 0)[None, :]
    sin_up = jnp.where(first, -sin, 0.0)
    sin_dn = jnp.where(first, 0.0, sin)
    rep = LANES // dim
    return tuple(jnp.tile(tab, (1, rep)) for tab in (cos, sin_up, sin_dn))


def _group_matrix(group):
    idx = jnp.arange(256) // group
    return (idx[:, None] == idx[None, :]).astype(BF16)


def kernel(x_prompt, x_sample, cache_gqa_k, cache_gqa_v, cache_diff_k, cache_diff_v, c, c_ctx,
           w_mod, b_mod, norm1_g, w_in, gqa_qn_g, gqa_kn_g, conv_w, conv_b, diff_qn_g, diff_kn_g,
           diff_lambda, diff_subln_g, w_out, norm2_g, ffn_up, ffn_conv_w, ffn_conv_b, ffn_down):
    depth = w_in.shape[0]
    batch, seq, _ = x_prompt.shape
    dec_batch, dec_seq, _ = x_sample.shape

    cond_all = jnp.zeros((MOD_ROWS, D_MODEL), F32).at[0].set(c_ctx).at[1:1 + dec_batch].set(c)
    mods = _mods_call(cond_all, w_mod, b_mod)
    kkc, vogc, kdc, vodc = _cache_prep_call(cache_gqa_k, cache_gqa_v, cache_diff_k, cache_diff_v)

    w_in_b = w_in.astype(BF16)
    w_out_b = w_out.astype(BF16)
    up_b = ffn_up.astype(BF16)
    down_b = ffn_down.astype(BF16)
    g64, g32 = _group_matrix(HEAD_DIM), _group_matrix(DIFF_QK_DIM)
    rope_tabs = _rope_tables(dec_seq // GRID_W, HEAD_DIM) + _rope_tables(dec_seq // GRID_W, DIFF_QK_DIM)

    def layer(x, l, mod_row, rope, cache_g, cache_d, own, tm, tq):
        lam_init = 0.8 - 0.6 * math.exp(-0.3 * l)
        gq = jnp.tile(gqa_qn_g[l] * (HEAD_DIM ** -0.5), 2).reshape(1, LANES)
        gk = jnp.tile(gqa_kn_g[l], 2).reshape(1, LANES)
        gqd = jnp.tile(diff_qn_g[l] * (DIFF_QK_DIM ** -0.5), 4).reshape(1, LANES)
        gkd = jnp.tile(diff_kn_g[l], 4).reshape(1, LANES)
        pre = _pre_call(x, mods[l], mod_row, norm1_g[l].reshape(1, -1), w_in_b[l], gq, gk, gqd, gkd,
                        g64, g32, rope, own, tm)
        qg, kk, vog, qd, kd, vod, cb, ccu = pre[:8]
        og = _gqa_call(qg, kk, vog, cache_g, l, tq)
        subg = jnp.tile(diff_subln_g[l], 2).reshape(1, LANES)
        od = _diff_call(qd, kd, vod, cache_d, l, diff_lambda[l], subg, lam_init, tq)
        x = _post_call(x, og, od, cb, ccu, mods[l], mod_row, conv_w[l], conv_b[l].reshape(1, -1),
                       w_out_b[l], norm2_g[l].reshape(1, -1), up_b[l], ffn_conv_w[l],
                       ffn_conv_b[l].reshape(1, -1), down_b[l], tm)
        return x, pre[8:]

    xp = x_prompt
    owns = []
    for l in range(depth):
        xp, own = layer(xp, l, lambda i: 0, None, None, None, True, seq, seq)
        owns.append(own)
    new_gqa_k = jnp.stack([o[0] for o in owns], axis=1).reshape(batch, depth, seq, GQA_KV_HEADS, HEAD_DIM)
    new_gqa_v = jnp.stack([o[1] for o in owns], axis=1).reshape(batch, depth, seq, GQA_KV_HEADS, HEAD_DIM)
    new_diff_k = jnp.stack([o[2] for o in owns], axis=1).reshape(batch, depth, seq, DIFF_HEADS, 2, DIFF_QK_DIM)
    new_diff_v = jnp.stack([o[3] for o in owns], axis=1).reshape(batch, depth, seq, DIFF_HEADS, DIFF_V_DIM)

    xs = x_sample
    for l in range(depth):
        xs, _ = layer(xs, l, lambda i: i + 1, rope_tabs, (kkc, vogc), (kdc, vodc), False, 512, 128)

    return (xp, xs, new_gqa_k, new_gqa_v, new_diff_k, new_diff_v)
```

```python
import functools
import math

import jax
import jax.numpy as jnp
from jax import lax
from jax.experimental import pallas as pl
from jax.experimental.pallas import tpu as pltpu

F32 = jnp.float32
BF16 = jnp.bfloat16

D_MODEL = 1024
GRID_W = 64
HEAD_DIM = 64
GQA_HEADS = 8
GQA_KV_HEADS = 2
GQA_WIDTH = GQA_HEADS * HEAD_DIM
CONV_WIDTH = 256
DIFF_HEADS = 4
DIFF_QK_DIM = 32
DIFF_V_DIM = 64
DIFF_WIDTH = DIFF_HEADS * DIFF_V_DIM
IN_WIDTH = 2304
D_FF = 2816
ROPE_THETA = 10000.0
NORM_EPS = 1e-6
N_MOD = 6
LANES = 128
MOD_ROWS = 16

OFF_QG, OFF_KG, OFF_VG = 0, 512, 640
OFF_CB, OFF_CC, OFF_CU = 768, 1024, 1280
OFF_QD, OFF_KD, OFF_VD = 1536, 1792, 2048

HALO = 16
PAD = 8

VMEM_LIMIT = 60 * 1024 * 1024

KEY_TILE = 256
SAFE_EXPONENT = 60.0
LOG2E = math.log2(math.e)


def _cparams():
    return pltpu.CompilerParams(vmem_limit_bytes=VMEM_LIMIT)


def _const_spec(shape):
    nd = len(shape)
    return pl.BlockSpec(shape, lambda *_: (0,) * nd, pipeline_mode=pl.Buffered(1))


def _mods_kernel(cond_ref, w_ref, b_ref, o_ref):
    cnd = cond_ref[...]
    act = cnd * jax.nn.sigmoid(cnd)
    o_ref[0] = jnp.dot(act, w_ref[0], preferred_element_type=F32,
                       precision=lax.Precision.HIGHEST) + b_ref[0]


def _mods_call(cond_all, w_mod, b_mod):
    depth = w_mod.shape[0]
    width = N_MOD * D_MODEL
    tn = 1024
    out = pl.pallas_call(
        _mods_kernel,
        out_shape=jax.ShapeDtypeStruct((depth, MOD_ROWS, width), F32),
        grid=(depth, width // tn),
        in_specs=[
            pl.BlockSpec((MOD_ROWS, D_MODEL), lambda l, j: (0, 0)),
            pl.BlockSpec((1, D_MODEL, tn), lambda l, j: (l, 0, j)),
            pl.BlockSpec((1, 1, tn), lambda l, j: (l, 0, j)),
        ],
        out_specs=pl.BlockSpec((1, MOD_ROWS, tn), lambda l, j: (l, 0, j)),
        compiler_params=_cparams(),
        name="adaln_mods",
    )(cond_all, w_mod, b_mod.reshape(depth, 1, width))
    return out.reshape(depth, MOD_ROWS, N_MOD, D_MODEL)


def _dup_halves(x128):
    lo = lax.broadcasted_iota(jnp.int32, x128.shape, 1) < HEAD_DIM
    swapped = pltpu.roll(x128, HEAD_DIM, 1)
    return jnp.where(lo, x128, swapped), jnp.where(lo, swapped, x128)


def _cache_prep_kernel(gk_ref, gv_ref, dk_ref, dv_ref, kk_ref, vog_ref, kd_ref, vod_ref):
    ones = jnp.ones((gk_ref.shape[2], LANES), BF16)
    k0, k1 = _dup_halves(gk_ref[0, 0])
    kk_ref[0, 0, 0] = k0.astype(BF16)
    kk_ref[0, 0, 1] = k1.astype(BF16)
    v0, v1 = _dup_halves(gv_ref[0, 0])
    vog_ref[0, 0, 0, :, :LANES] = v0.astype(BF16)
    vog_ref[0, 0, 1, :, :LANES] = v1.astype(BF16)
    vog_ref[0, 0, 0, :, LANES:] = ones
    vog_ref[0, 0, 1, :, LANES:] = ones
    kd_ref[0, 0] = dk_ref[0, 0].astype(BF16)
    dv = dv_ref[0, 0]
    vod_ref[0, 0, 0, :, :LANES] = dv[:, :LANES].astype(BF16)
    vod_ref[0, 0, 1, :, :LANES] = dv[:, LANES:].astype(BF16)
    vod_ref[0, 0, 0, :, LANES:] = ones
    vod_ref[0, 0, 1, :, LANES:] = ones


def _cache_prep_call(cache_gqa_k, cache_gqa_v, cache_diff_k, cache_diff_v):
    b, depth, s = cache_gqa_k.shape[:3]
    gk = cache_gqa_k.reshape(b, depth, s, LANES)
    gv = cache_gqa_v.reshape(b, depth, s, LANES)
    dk = cache_diff_k.reshape(b, depth, s, 2 * LANES)
    dv = cache_diff_v.reshape(b, depth, s, 2 * LANES)
    idx4 = lambda i, l: (i, l, 0, 0)
    idx5 = lambda i, l: (i, l, 0, 0, 0)
    return pl.pallas_call(
        _cache_prep_kernel,
        out_shape=(
            jax.ShapeDtypeStruct((b, depth, 2, s, LANES), BF16),
            jax.ShapeDtypeStruct((b, depth, 2, s, 2 * LANES), BF16),
            jax.ShapeDtypeStruct((b, depth, s, 2 * LANES), BF16),
            jax.ShapeDtypeStruct((b, depth, 2, s, 2 * LANES), BF16),
        ),
        grid=(b, depth),
        in_specs=[
            pl.BlockSpec((1, 1, s, LANES), idx4),
            pl.BlockSpec((1, 1, s, LANES), idx4),
            pl.BlockSpec((1, 1, s, 2 * LANES), idx4),
            pl.BlockSpec((1, 1, s, 2 * LANES), idx4),
        ],
        out_specs=(
            pl.BlockSpec((1, 1, 2, s, LANES), idx5),
            pl.BlockSpec((1, 1, 2, s, 2 * LANES), idx5),
            pl.BlockSpec((1, 1, s, 2 * LANES), idx4),
            pl.BlockSpec((1, 1, 2, s, 2 * LANES), idx5),
        ),
        compiler_params=_cparams(),
        name="cache_prep",
    )(gk, gv, dk, dv)


def _group_inv_rms(z, g_ref, group):
    sq = z * z
    hi = sq.astype(BF16)
    lo = (sq - hi.astype(F32)).astype(BF16)
    width = z.shape[1]
    gmat = g_ref[:width, :width]
    ssum = (jnp.dot(hi, gmat, preferred_element_type=F32)
            + jnp.dot(lo, gmat, preferred_element_type=F32))
    return lax.rsqrt(ssum * (1.0 / group) + NORM_EPS)


def _rope_chunk(y, cos, sin_up, sin_dn, quarter):
    up = pltpu.roll(y, LANES - quarter, 1)
    dn = pltpu.roll(y, quarter, 1)
    return y * cos + up * sin_up + dn * sin_dn


def _pre_kernel(*refs, rope, own):
    it = iter(refs)
    x_ref, mod_ref, n1g_ref, w_in_ref = next(it), next(it), next(it), next(it)
    gq_ref, gk_ref, gqd_ref, gkd_ref = next(it), next(it), next(it), next(it)
    g64_ref, g32_ref = next(it), next(it)
    if rope:
        c64, su64, sd64, c32, su32, sd32 = (next(it) for _ in range(6))
    qg_ref, kk_ref, vog_ref, qd_ref, kd_ref, vod_ref, cb_ref, ccu_ref = (next(it) for _ in range(8))
    if own:
        kown_ref, vown_ref, kdown_ref, vdown_ref = (next(it) for _ in range(4))

    x = x_ref[0]
    mods = mod_ref[0]
    sh1, sc1 = mods[0:1], mods[1:2]
    inv = lax.rsqrt(jnp.mean(x * x, axis=-1, keepdims=True) + NORM_EPS)
    h = (x * inv) * n1g_ref[...] * (1.0 + sc1) + sh1
    z = jnp.dot(h.astype(BF16), w_in_ref[...], preferred_element_type=F32)
    tm = x.shape[0]

    def rope64(y):
        if not rope:
            return y
        return _rope_chunk(y, c64[...], su64[...], sd64[...], HEAD_DIM // 4)

    def rope32(y):
        if not rope:
            return y
        return _rope_chunk(y, c32[...], su32[...], sd32[...], DIFF_QK_DIM // 4)

    for c in range(2):
        zc = z[:, OFF_QG + 256 * c: OFF_QG + 256 * (c + 1)]
        y = zc * _group_inv_rms(zc, g64_ref, HEAD_DIM)
        for j in range(2):
            yj = rope64(y[:, LANES * j: LANES * (j + 1)] * gq_ref[...])
            qg_ref[0, :, 256 * c + LANES * j: 256 * c + LANES * (j + 1)] = yj.astype(BF16)

    zk = z[:, OFF_KG: OFF_KG + LANES]
    zv = z[:, OFF_VG: OFF_VG + LANES]
    kn = zk * _group_inv_rms(zk, g64_ref, HEAD_DIM) * gk_ref[...]
    if own:
        kown_ref[0] = kn
        vown_ref[0] = zv
    k0, k1 = _dup_halves(rope64(kn))
    kk_ref[0, 0] = k0.astype(BF16)
    kk_ref[0, 1] = k1.astype(BF16)
    v0, v1 = _dup_halves(zv)
    ones = jnp.ones((tm, LANES), BF16)
    vog_ref[0, 0, :, :LANES] = v0.astype(BF16)
    vog_ref[0, 1, :, :LANES] = v1.astype(BF16)
    vog_ref[0, 0, :, LANES:] = ones
    vog_ref[0, 1, :, LANES:] = ones

    cb_ref[0] = z[:, OFF_CB: OFF_CB + CONV_WIDTH]
    ccu_ref[0] = z[:, OFF_CC: OFF_CC + CONV_WIDTH] * z[:, OFF_CU: OFF_CU + CONV_WIDTH]

    zq = z[:, OFF_QD: OFF_QD + 256]
    yq = zq * _group_inv_rms(zq, g32_ref, DIFF_QK_DIM)
    zkd = z[:, OFF_KD: OFF_KD + 256]
    ykd = zkd * _group_inv_rms(zkd, g32_ref, DIFF_QK_DIM)
    zvd = z[:, OFF_VD: OFF_VD + 256]
    for j in range(2):
        sl = slice(LANES * j, LANES * (j + 1))
        qd_ref[0, :, sl] = rope32(yq[:, sl] * gqd_ref[...]).astype(BF16)
        kdj = ykd[:, sl] * gkd_ref[...]
        if own:
            kdown_ref[0, :, sl] = kdj
        kd_ref[0, :, sl] = rope32(kdj).astype(BF16)
        vod_ref[0, j, :, :LANES] = zvd[:, sl].astype(BF16)
        vod_ref[0, j, :, LANES:] = ones
    if own:
        vdown_ref[0] = zvd


def _pre_call(x, mods_l, mod_row, n1g, w_in, gq, gk, gqd, gkd, g64, g32, rope_tabs, own, tm):
    b, t, d = x.shape
    nt = t // tm
    rope = rope_tabs is not None
    in_specs = [
        pl.BlockSpec((1, tm, d), lambda i, j: (i, j, 0)),
        pl.BlockSpec((1, N_MOD, d), lambda i, j: (mod_row(i), 0, 0)),
        _const_spec((1, d)),
        _const_spec((d, IN_WIDTH)),
        _const_spec((1, LANES)), _const_spec((1, LANES)), _const_spec((1, LANES)), _const_spec((1, LANES)),
        _const_spec((256, 256)), _const_spec((256, 256)),
    ]
    args = [x, mods_l, n1g, w_in, gq, gk, gqd, gkd, g64, g32]
    if rope:
        in_specs += [pl.BlockSpec((tm, LANES), lambda i, j: (j, 0))] * 6
        args += list(rope_tabs)
    tok = lambda w, dt: jax.ShapeDtypeStruct((b, t, w), dt)
    tok_spec = lambda w: pl.BlockSpec((1, tm, w), lambda i, j: (i, j, 0))
    pair = lambda w: jax.ShapeDtypeStruct((b, 2, t, w), BF16)
    pair_spec = lambda w: pl.BlockSpec((1, 2, tm, w), lambda i, j: (i, 0, j, 0))
    out_shape = [tok(GQA_WIDTH, BF16), pair(LANES), pair(2 * LANES), tok(256, BF16), tok(256, BF16),
                 pair(2 * LANES), tok(CONV_WIDTH, F32), tok(CONV_WIDTH, F32)]
    out_specs = [tok_spec(GQA_WIDTH), pair_spec(LANES), pair_spec(2 * LANES), tok_spec(256), tok_spec(256),
                 pair_spec(2 * LANES), tok_spec(CONV_WIDTH), tok_spec(CONV_WIDTH)]
    if own:
        out_shape += [tok(LANES, F32), tok(LANES, F32), tok(256, F32), tok(256, F32)]
        out_specs += [tok_spec(LANES), tok_spec(LANES), tok_spec(256), tok_spec(256)]
    return pl.pallas_call(
        functools.partial(_pre_kernel, rope=rope, own=own),
        out_shape=tuple(out_shape),
        grid=(b, nt),
        in_specs=in_specs,
        out_specs=tuple(out_specs),
        compiler_params=_cparams(),
        name="pre_attention",
    )(*args)


def _key_tiles(ref, lead, n_rows):
    def load(t):
        return ref[lead + (pl.ds(t * KEY_TILE, KEY_TILE), slice(None))]
    return [functools.partial(load, t) for t in range(n_rows // KEY_TILE)]


def _row_norm2_max(x):
    xf = x.astype(F32)
    return jnp.max(jnp.sum(xf * xf, axis=-1, keepdims=True))


def _dot_nt(a, b):
    return lax.dot_general(a, b, (((1,), (1,)), ((), ())), preferred_element_type=F32)


def _attend(lhs, k_tiles, v_tiles, qb2_ref, safe_ref, r_ref, first_q_tile):
    @pl.when(first_q_tile)
    def _():
        kmax2 = functools.reduce(jnp.maximum, [_row_norm2_max(k()) for k in k_tiles])
        safe_ref[0] = (qb2_ref[0] * kmax2 <= SAFE_EXPONENT ** 2).astype(jnp.int32)

    safe = safe_ref[0] == 1

    @pl.when(safe)
    def _():
        acc = None
        for k, v in zip(k_tiles, v_tiles):
            p = jnp.exp2(_dot_nt(lhs, k())).astype(BF16)
            part = jnp.dot(p, v(), preferred_element_type=F32)
            acc = part if acc is None else acc + part
        r_ref[...] = acc[:, :LANES] / acc[:, LANES:]

    @pl.when(jnp.logical_not(safe))
    def _():
        scores = [_dot_nt(lhs, k()) for k in k_tiles]
        m = functools.reduce(jnp.maximum, [jnp.max(s, axis=-1, keepdims=True) for s in scores])
        acc = None
        for s, v in zip(scores, v_tiles):
            p = jnp.exp2(s - m).astype(BF16)
            part = jnp.dot(p, v(), preferred_element_type=F32)
            acc = part if acc is None else acc + part
        r_ref[...] = acc[:, :LANES] / acc[:, LANES:]


def _attn_scratch(tq):
    return [pltpu.VMEM((4 * tq, LANES), F32), pltpu.SMEM((1,), jnp.int32)]


def _query_norm_bound(gain, dim):
    return (dim * 1.02 * jnp.max(gain * gain)).reshape(1)


_SMEM_SPEC = pl.BlockSpec(memory_space=pltpu.SMEM)


def _attn_cparams():
    return pltpu.CompilerParams(vmem_limit_bytes=VMEM_LIMIT,
                                dimension_semantics=("arbitrary", "arbitrary", "arbitrary"))


def _gqa_kernel(*refs, has_cache):
    if has_cache:
        q_ref, kk_ref, vo_ref, kkc_ref, voc_ref, qb2_ref, o_ref, r_ref, safe_ref = refs
        k_tiles = _key_tiles(kk_ref, (0, 0), kk_ref.shape[2]) + _key_tiles(kkc_ref, (0, 0, 0), kkc_ref.shape[3])
        v_tiles = _key_tiles(vo_ref, (0, 0), vo_ref.shape[2]) + _key_tiles(voc_ref, (0, 0, 0), voc_ref.shape[3])
    else:
        q_ref, kk_ref, vo_ref, qb2_ref, o_ref, r_ref, safe_ref = refs
        k_tiles = _key_tiles(kk_ref, (0, 0), kk_ref.shape[2])
        v_tiles = _key_tiles(vo_ref, (0, 0), vo_ref.shape[2])
    q = q_ref[0]
    tq = q.shape[0]
    lo = lax.broadcasted_iota(jnp.int32, (tq, LANES), 1) < HEAD_DIM
    zero = jnp.zeros((tq, LANES), BF16)
    qa, qb = q[:, :LANES], q[:, LANES:]
    lhs = jnp.concatenate([jnp.where(lo, qa, zero), jnp.where(lo, zero, qa),
                           jnp.where(lo, qb, zero), jnp.where(lo, zero, qb)], axis=0)
    _attend(lhs, k_tiles, v_tiles, qb2_ref, safe_ref, r_ref, pl.program_id(2) == 0)
    oa = jnp.where(lo, r_ref[0:tq], r_ref[tq:2 * tq])
    ob = jnp.where(lo, r_ref[2 * tq:3 * tq], r_ref[3 * tq:4 * tq])
    o_ref[0, :, :LANES] = oa.astype(BF16)
    o_ref[0, :, LANES:] = ob.astype(BF16)


def _gqa_call(qg, kk, vog, cache, layer, qb2, tq):
    b, t, _ = qg.shape
    nq = t // tq
    in_specs = [
        pl.BlockSpec((1, tq, 256), lambda i, h, j: (i, j, h)),
        pl.BlockSpec((1, 1, t, LANES), lambda i, h, j: (i, h, 0, 0)),
        pl.BlockSpec((1, 1, t, 2 * LANES), lambda i, h, j: (i, h, 0, 0)),
    ]
    args = [qg, kk, vog]
    if cache is not None:
        kkc, voc = cache
        s = kkc.shape[3]
        in_specs += [
            pl.BlockSpec((1, 1, 1, s, LANES), lambda i, h, j: (i, layer, h, 0, 0)),
            pl.BlockSpec((1, 1, 1, s, 2 * LANES), lambda i, h, j: (i, layer, h, 0, 0)),
        ]
        args += [kkc, voc]
    in_specs.append(_SMEM_SPEC)
    args.append(qb2)
    return pl.pallas_call(
        functools.partial(_gqa_kernel, has_cache=cache is not None),
        out_shape=jax.ShapeDtypeStruct((b, t, GQA_WIDTH), BF16),
        grid=(b, GQA_KV_HEADS, nq),
        in_specs=in_specs,
        out_specs=pl.BlockSpec((1, tq, 256), lambda i, h, j: (i, j, h)),
        scratch_shapes=_attn_scratch(tq),
        compiler_params=_attn_cparams(),
        name="gqa_attention",
    )(*args)


def _diff_kernel(*refs, has_cache, lam_init):
    if has_cache:
        q_ref, k_ref, vo_ref, kc_ref, voc_ref, lam_ref, sg_ref, qb2_ref, o_ref, r_ref, safe_ref = refs
        k_tiles = _key_tiles(k_ref, (0,), k_ref.shape[1]) + _key_tiles(kc_ref, (0, 0), kc_ref.shape[2])
        v_tiles = _key_tiles(vo_ref, (0, 0), vo_ref.shape[2]) + _key_tiles(voc_ref, (0, 0, 0), voc_ref.shape[3])
    else:
        q_ref, k_ref, vo_ref, lam_ref, sg_ref, qb2_ref, o_ref, r_ref, safe_ref = refs
        k_tiles = _key_tiles(k_ref, (0,), k_ref.shape[1])
        v_tiles = _key_tiles(vo_ref, (0, 0), vo_ref.shape[2])
    q = q_ref[0]
    tq = q.shape[0]
    lane = lax.broadcasted_iota(jnp.int32, (tq, LANES), 1)
    zero = jnp.zeros((tq, LANES), BF16)
    lhs = jnp.concatenate(
        [jnp.where((lane >= DIFF_QK_DIM * u) & (lane < DIFF_QK_DIM * (u + 1)), q, zero) for u in range(4)],
        axis=0)
    _attend(lhs, k_tiles, v_tiles, qb2_ref, safe_ref, r_ref, pl.program_id(2) == 0)
    lf = lam_ref[...]
    lam = (jnp.exp(jnp.sum(lf[0:1] * lf[1:2], axis=-1, keepdims=True))
           - jnp.exp(jnp.sum(lf[2:3] * lf[3:4], axis=-1, keepdims=True)) + lam_init)
    lo = lane < DIFF_V_DIM
    o = jnp.where(lo, r_ref[0:tq] - lam * r_ref[tq:2 * tq], r_ref[2 * tq:3 * tq] - lam * r_ref[3 * tq:4 * tq])
    sq = o * o
    ms_lo = jnp.sum(jnp.where(lo, sq, 0.0), axis=-1, keepdims=True) * (1.0 / DIFF_V_DIM)
    ms_hi = jnp.sum(jnp.where(lo, 0.0, sq), axis=-1, keepdims=True) * (1.0 / DIFF_V_DIM)
    inv = jnp.where(lo, lax.rsqrt(ms_lo + NORM_EPS), lax.rsqrt(ms_hi + NORM_EPS))
    o_ref[0] = ((o * inv) * sg_ref[...] * (1.0 - lam_init)).astype(BF16)


def _diff_call(qd, kd, vod, cache, layer, lam_p, subg, qb2, lam_init, tq):
    b, t, _ = qd.shape
    nq = t // tq
    in_specs = [
        pl.BlockSpec((1, tq, LANES), lambda i, h, j: (i, j, h)),
        pl.BlockSpec((1, t, LANES), lambda i, h, j: (i, 0, h)),
        pl.BlockSpec((1, 1, t, 2 * LANES), lambda i, h, j: (i, h, 0, 0)),
    ]
    args = [qd, kd, vod]
    if cache is not None:
        kdc, vodc = cache
        s = kdc.shape[2]
        in_specs += [
            pl.BlockSpec((1, 1, s, LANES), lambda i, h, j: (i, layer, 0, h)),
            pl.BlockSpec((1, 1, 1, s, 2 * LANES), lambda i, h, j: (i, layer, h, 0, 0)),
        ]
        args += [kdc, vodc]
    in_specs += [pl.BlockSpec((4, DIFF_QK_DIM), lambda i, h, j: (0, 0)),
                 pl.BlockSpec((1, LANES), lambda i, h, j: (0, 0)), _SMEM_SPEC]
    args += [lam_p, subg, qb2]
    return pl.pallas_call(
        functools.partial(_diff_kernel, has_cache=cache is not None, lam_init=lam_init),
        out_shape=jax.ShapeDtypeStruct((b, t, DIFF_WIDTH), BF16),
        grid=(b, 2, nq),
        in_specs=in_specs,
        out_specs=pl.BlockSpec((1, tq, LANES), lambda i, h, j: (i, j, h)),
        scratch_shapes=_attn_scratch(tq),
        compiler_params=_attn_cparams(),
        name="diff_attention",
    )(*args)


def _post_kernel(*refs, halo, tm, t_len):
    it = iter(refs)
    n_blk = 3 if halo else 1

    def take():
        blocks = [next(it) for _ in range(n_blk)]
        if halo:
            return jnp.concatenate([blocks[0][0], blocks[1][0], blocks[2][0]], axis=0)
        return blocks[0][0]

    xe, og, od, cb, ccu = take(), take(), take(), take(), take()
    mod_ref, cw_ref, cbias_ref, wout_ref, n2g_ref = (next(it) for _ in range(5))
    up_ref, fcw_ref, fcb_ref, down_ref = (next(it) for _ in range(4))
    out_ref, ccu_scr, a_scr = next(it), next(it), next(it)

    h0 = HALO if halo else 0
    rows = tm + 2 * h0
    mods = mod_ref[0]
    g1, sh2, sc2, g2 = mods[2:3], mods[3:4], mods[4:5], mods[5:6]

    if halo:
        pos = pl.program_id(1) * tm - h0 + lax.broadcasted_iota(jnp.int32, (rows, 1), 0)
        valid = (pos >= 0) & (pos < t_len)
        ccu = jnp.where(valid, ccu, 0.0)

    ccu_scr[0:PAD] = jnp.zeros((PAD, CONV_WIDTH), F32)
    ccu_scr[PAD + rows: 2 * PAD + rows] = jnp.zeros((PAD, CONV_WIDTH), F32)
    ccu_scr[PAD: PAD + rows] = ccu
    cw = cw_ref[...]
    conv = (ccu_scr[PAD - 1: PAD - 1 + rows] * cw[0:1] + ccu * cw[1:2]
            + ccu_scr[PAD + 1: PAD + 1 + rows] * cw[2:3] + cbias_ref[...])
    oc = (cb * conv).astype(BF16)

    y = (jnp.dot(og, wout_ref[0:GQA_WIDTH], preferred_element_type=F32)
         + jnp.dot(oc, wout_ref[GQA_WIDTH: GQA_WIDTH + CONV_WIDTH], preferred_element_type=F32)
         + jnp.dot(od, wout_ref[GQA_WIDTH + CONV_WIDTH:], preferred_element_type=F32))
    xmid = xe + g1 * y

    inv = lax.rsqrt(jnp.mean(xmid * xmid, axis=-1, keepdims=True) + NORM_EPS)
    h2 = ((xmid * inv) * n2g_ref[...] * (1.0 + sc2) + sh2).astype(BF16)
    a = jnp.dot(h2, up_ref[:, :D_FF], preferred_element_type=F32)
    if halo:
        a = jnp.where(valid, a, 0.0)
    a_scr[0:PAD] = jnp.zeros((PAD, D_FF), F32)
    a_scr[PAD + rows: 2 * PAD + rows] = jnp.zeros((PAD, D_FF), F32)
    a_scr[PAD: PAD + rows] = a
    base = PAD + h0
    fcw = fcw_ref[...]
    ac = (a_scr[base - 1: base - 1 + tm] * fcw[0:1] + a_scr[base: base + tm] * fcw[1:2]
          + a_scr[base + 1: base + 1 + tm] * fcw[2:3] + fcb_ref[...])
    u = jnp.dot(h2[h0: h0 + tm], up_ref[:, D_FF:], preferred_element_type=F32)
    f = (ac * jax.nn.sigmoid(ac) * u).astype(BF16)
    out_ref[0] = xmid[h0: h0 + tm] + g2 * jnp.dot(f, down_ref[...], preferred_element_type=F32)


def _post_call(x, og, od, cb, ccu, mods_l, mod_row, cw, cbias, w_out, n2g, up, fcw, fcb, down, tm):
    b, t, d = x.shape
    nt = t // tm
    halo = nt > 1
    per = tm // HALO
    last = t // HALO - 1

    def specs(width):
        main = pl.BlockSpec((1, tm, width), lambda i, j: (i, j, 0))
        if not halo:
            return [main]
        prev = pl.BlockSpec((1, HALO, width), lambda i, j: (i, jnp.maximum(j * per - 1, 0), 0))
        nxt = pl.BlockSpec((1, HALO, width), lambda i, j: (i, jnp.minimum((j + 1) * per, last), 0))
        return [prev, main, nxt]

    in_specs, args = [], []
    for arr in (x, og, od, cb, ccu):
        sp = specs(arr.shape[-1])
        in_specs += sp
        args += [arr] * len(sp)
    in_specs += [
        pl.BlockSpec((1, N_MOD, d), lambda i, j: (mod_row(i), 0, 0)),
        _const_spec((3, CONV_WIDTH)), _const_spec((1, CONV_WIDTH)),
        _const_spec((d, d)), _const_spec((1, d)),
        _const_spec((d, 2 * D_FF)), _const_spec((3, D_FF)), _const_spec((1, D_FF)),
        _const_spec((D_FF, d)),
    ]
    args += [mods_l, cw, cbias, w_out, n2g, up, fcw, fcb, down]
    rows = tm + (2 * HALO if halo else 0)
    return pl.pallas_call(
        functools.partial(_post_kernel, halo=halo, tm=tm, t_len=t),
        out_shape=jax.ShapeDtypeStruct((b, t, d), F32),
        grid=(b, nt),
        in_specs=in_specs,
        out_specs=pl.BlockSpec((1, tm, d), lambda i, j: (i, j, 0)),
        scratch_shapes=[pltpu.VMEM((rows + 2 * PAD, CONV_WIDTH), F32),
                        pltpu.VMEM((rows + 2 * PAD, D_FF), F32)],
        compiler_params=_cparams(),
        name="post_attention",
    )(*args)


def _rope_tables(n_rows, dim):
    row = jnp.repeat(jnp.arange(n_rows), GRID_W).astype(F32)
    col = jnp.tile(jnp.arange(GRID_W), n_rows).astype(F32)
    nf = dim // 4
    freqs = ROPE_THETA ** (-jnp.arange(nf, dtype=F32) / nf)
    ar = row[:, None] * freqs[None, :]
    ac = col[:, None] * freqs[None, :]
    ang = jnp.concatenate([ar, ar, ac, ac], axis=-1)
    cos, sin = jnp.cos(ang), jnp.sin(ang)
    first = ((jnp.arange(dim) // nf) % 2 == 0)[None, :]
    sin_up = jnp.where(first, -sin, 0.0)
    sin_dn = jnp.where(first, 0.0, sin)
    rep = LANES // dim
    return tuple(jnp.tile(tab, (1, rep)) for tab in (cos, sin_up, sin_dn))


def _group_matrix(group):
    idx = jnp.arange(256) // group
    return (idx[:, None] == idx[None, :]).astype(BF16)


def kernel(x_prompt, x_sample, cache_gqa_k, cache_gqa_v, cache_diff_k, cache_diff_v, c, c_ctx,
           w_mod, b_mod, norm1_g, w_in, gqa_qn_g, gqa_kn_g, conv_w, conv_b, diff_qn_g, diff_kn_g,
           diff_lambda, diff_subln_g, w_out, norm2_g, ffn_up, ffn_conv_w, ffn_conv_b, ffn_down):
    depth = w_in.shape[0]
    batch, seq, _ = x_prompt.shape
    dec_batch, dec_seq, _ = x_sample.shape

    cond_all = jnp.zeros((MOD_ROWS, D_MODEL), F32).at[0].set(c_ctx).at[1:1 + dec_batch].set(c)
    mods = _mods_call(cond_all, w_mod, b_mod)
    kkc, vogc, kdc, vodc = _cache_prep_call(cache_gqa_k, cache_gqa_v, cache_diff_k, cache_diff_v)

    w_in_b = w_in.astype(BF16)
    w_out_b = w_out.astype(BF16)
    up_b = ffn_up.astype(BF16)
    down_b = ffn_down.astype(BF16)
    g64, g32 = _group_matrix(HEAD_DIM), _group_matrix(DIFF_QK_DIM)
    rope_tabs = _rope_tables(dec_seq // GRID_W, HEAD_DIM) + _rope_tables(dec_seq // GRID_W, DIFF_QK_DIM)

    def layer(x, l, mod_row, rope, cache_g, cache_d, own, tm, tq):
        lam_init = 0.8 - 0.6 * math.exp(-0.3 * l)
        gq = jnp.tile(gqa_qn_g[l] * (HEAD_DIM ** -0.5 * LOG2E), 2).reshape(1, LANES)
        gk = jnp.tile(gqa_kn_g[l], 2).reshape(1, LANES)
        gqd = jnp.tile(diff_qn_g[l] * (DIFF_QK_DIM ** -0.5 * LOG2E), 4).reshape(1, LANES)
        gkd = jnp.tile(diff_kn_g[l], 4).reshape(1, LANES)
        pre = _pre_call(x, mods[l], mod_row, norm1_g[l].reshape(1, -1), w_in_b[l], gq, gk, gqd, gkd,
                        g64, g32, rope, own, tm)
        qg, kk, vog, qd, kd, vod, cb, ccu = pre[:8]
        og = _gqa_call(qg, kk, vog, cache_g, l, _query_norm_bound(gq, HEAD_DIM), tq)
        subg = jnp.tile(diff_subln_g[l], 2).reshape(1, LANES)
        od = _diff_call(qd, kd, vod, cache_d, l, diff_lambda[l], subg,
                        _query_norm_bound(gqd, DIFF_QK_DIM), lam_init, tq)
        x = _post_call(x, og, od, cb, ccu, mods[l], mod_row, conv_w[l], conv_b[l].reshape(1, -1),
                       w_out_b[l], norm2_g[l].reshape(1, -1), up_b[l], ffn_conv_w[l],
                       ffn_conv_b[l].reshape(1, -1), down_b[l], tm)
        return x, pre[8:]

    xp = x_prompt
    owns = []
    for l in range(depth):
        xp, own = layer(xp, l, lambda i: 0, None, None, None, True, seq, seq)
        owns.append(own)
    new_gqa_k = jnp.stack([o[0] for o in owns], axis=1).reshape(batch, depth, seq, GQA_KV_HEADS, HEAD_DIM)
    new_gqa_v = jnp.stack([o[1] for o in owns], axis=1).reshape(batch, depth, seq, GQA_KV_HEADS, HEAD_DIM)
    new_diff_k = jnp.stack([o[2] for o in owns], axis=1).reshape(batch, depth, seq, DIFF_HEADS, 2, DIFF_QK_DIM)
    new_diff_v = jnp.stack([o[3] for o in owns], axis=1).reshape(batch, depth, seq, DIFF_HEADS, DIFF_V_DIM)

    xs = x_sample
    for l in range(depth):
        xs, _ = layer(xs, l, lambda i: i + 1, rope_tabs, (kkc, vogc), (kdc, vodc), False, 512, 128)

    return (xp, xs, new_gqa_k, new_gqa_v, new_diff_k, new_diff_v)
```

```python
import functools
import math

import jax
import jax.numpy as jnp
from jax import lax
from jax.experimental import pallas as pl
from jax.experimental.pallas import tpu as pltpu

F32 = jnp.float32
BF16 = jnp.bfloat16

D_MODEL = 1024
GRID_W = 64
HEAD_DIM = 64
GQA_HEADS = 8
GQA_KV_HEADS = 2
GQA_WIDTH = GQA_HEADS * HEAD_DIM
CONV_WIDTH = 256
DIFF_HEADS = 4
DIFF_QK_DIM = 32
DIFF_V_DIM = 64
DIFF_WIDTH = DIFF_HEADS * DIFF_V_DIM
IN_WIDTH = 2304
D_FF = 2816
ROPE_THETA = 10000.0
NORM_EPS = 1e-6
N_MOD = 6
LANES = 128
MOD_ROWS = 16

OFF_QG, OFF_KG, OFF_VG = 0, 512, 640
OFF_CB, OFF_CC, OFF_CU = 768, 1024, 1280
OFF_QD, OFF_KD, OFF_VD = 1536, 1792, 2048

HALO = 16
PAD = 8

VMEM_LIMIT = 60 * 1024 * 1024

KEY_TILE = 256
SAFE_EXPONENT = 60.0
LOG2E = math.log2(math.e)
PRE_ROWS = 128
EXACT_ROWS = 512


def _cparams():
    return pltpu.CompilerParams(vmem_limit_bytes=VMEM_LIMIT)


def _const_spec(shape):
    nd = len(shape)
    return pl.BlockSpec(shape, lambda *_: (0,) * nd, pipeline_mode=pl.Buffered(1))


def _layer_spec(shape, layer):
    nd = len(shape)
    return pl.BlockSpec((None,) + tuple(shape), lambda *_: (layer,) + (0,) * nd, pipeline_mode=pl.Buffered(1))


def _mods_spec(layer, mod_row):
    return pl.BlockSpec((None, 1, N_MOD, D_MODEL), lambda i, j: (layer, mod_row(i), 0, 0))


def _mods_kernel(cond_ref, w_ref, b_ref, o_ref):
    cnd = cond_ref[...]
    act = cnd * jax.nn.sigmoid(cnd)
    o_ref[0] = jnp.dot(act, w_ref[0], preferred_element_type=F32,
                       precision=lax.Precision.HIGHEST) + b_ref[0]


def _mods_call(cond_all, w_mod, b_mod):
    depth = w_mod.shape[0]
    width = N_MOD * D_MODEL
    tn = 1024
    out = pl.pallas_call(
        _mods_kernel,
        out_shape=jax.ShapeDtypeStruct((depth, MOD_ROWS, width), F32),
        grid=(depth, width // tn),
        in_specs=[
            pl.BlockSpec((MOD_ROWS, D_MODEL), lambda l, j: (0, 0)),
            pl.BlockSpec((1, D_MODEL, tn), lambda l, j: (l, 0, j)),
            pl.BlockSpec((1, 1, tn), lambda l, j: (l, 0, j)),
        ],
        out_specs=pl.BlockSpec((1, MOD_ROWS, tn), lambda l, j: (l, 0, j)),
        compiler_params=_cparams(),
        name="adaln_mods",
    )(cond_all, w_mod, b_mod.reshape(depth, 1, width))
    return out.reshape(depth, MOD_ROWS, N_MOD, D_MODEL)


def _dup_halves(x128):
    lo = lax.broadcasted_iota(jnp.int32, x128.shape, 1) < HEAD_DIM
    swapped = pltpu.roll(x128, HEAD_DIM, 1)
    return jnp.where(lo, x128, swapped), jnp.where(lo, swapped, x128)


def _cache_prep_kernel(gk_ref, gv_ref, dk_ref, dv_ref, kk_ref, vog_ref, kd_ref, vod_ref):
    ones = jnp.ones((gk_ref.shape[2], LANES), BF16)
    k0, k1 = _dup_halves(gk_ref[0, 0])
    kk_ref[0, 0, 0] = k0.astype(BF16)
    kk_ref[0, 0, 1] = k1.astype(BF16)
    v0, v1 = _dup_halves(gv_ref[0, 0])
    vog_ref[0, 0, 0, :, :LANES] = v0.astype(BF16)
    vog_ref[0, 0, 1, :, :LANES] = v1.astype(BF16)
    vog_ref[0, 0, 0, :, LANES:] = ones
    vog_ref[0, 0, 1, :, LANES:] = ones
    kd_ref[0, 0] = dk_ref[0, 0].astype(BF16)
    dv = dv_ref[0, 0]
    vod_ref[0, 0, 0, :, :LANES] = dv[:, :LANES].astype(BF16)
    vod_ref[0, 0, 1, :, :LANES] = dv[:, LANES:].astype(BF16)
    vod_ref[0, 0, 0, :, LANES:] = ones
    vod_ref[0, 0, 1, :, LANES:] = ones


def _cache_prep_call(cache_gqa_k, cache_gqa_v, cache_diff_k, cache_diff_v):
    b, depth, s = cache_gqa_k.shape[:3]
    gk = cache_gqa_k.reshape(b, depth, s, LANES)
    gv = cache_gqa_v.reshape(b, depth, s, LANES)
    dk = cache_diff_k.reshape(b, depth, s, 2 * LANES)
    dv = cache_diff_v.reshape(b, depth, s, 2 * LANES)
    idx4 = lambda i, l: (i, l, 0, 0)
    idx5 = lambda i, l: (i, l, 0, 0, 0)
    return pl.pallas_call(
        _cache_prep_kernel,
        out_shape=(
            jax.ShapeDtypeStruct((b, depth, 2, s, LANES), BF16),
            jax.ShapeDtypeStruct((b, depth, 2, s, 2 * LANES), BF16),
            jax.ShapeDtypeStruct((b, depth, s, 2 * LANES), BF16),
            jax.ShapeDtypeStruct((b, depth, 2, s, 2 * LANES), BF16),
        ),
        grid=(b, depth),
        in_specs=[
            pl.BlockSpec((1, 1, s, LANES), idx4),
            pl.BlockSpec((1, 1, s, LANES), idx4),
            pl.BlockSpec((1, 1, s, 2 * LANES), idx4),
            pl.BlockSpec((1, 1, s, 2 * LANES), idx4),
        ],
        out_specs=(
            pl.BlockSpec((1, 1, 2, s, LANES), idx5),
            pl.BlockSpec((1, 1, 2, s, 2 * LANES), idx5),
            pl.BlockSpec((1, 1, s, 2 * LANES), idx4),
            pl.BlockSpec((1, 1, 2, s, 2 * LANES), idx5),
        ),
        compiler_params=_cparams(),
        name="cache_prep",
    )(gk, gv, dk, dv)


def _group_inv_rms(z, g_ref, group):
    sq = z * z
    hi = sq.astype(BF16)
    lo = (sq - hi.astype(F32)).astype(BF16)
    width = z.shape[1]
    gmat = g_ref[:width, :width]
    ssum = (jnp.dot(hi, gmat, preferred_element_type=F32)
            + jnp.dot(lo, gmat, preferred_element_type=F32))
    return lax.rsqrt(ssum * (1.0 / group) + NORM_EPS)


def _rope_chunk(y, cos, sin_up, sin_dn, quarter):
    up = pltpu.roll(y, LANES - quarter, 1)
    dn = pltpu.roll(y, quarter, 1)
    return y * cos + up * sin_up + dn * sin_dn


def _pre_kernel(*refs, rope, own):
    it = iter(refs)
    x_ref, mod_ref, n1g_ref, w_in_ref = next(it), next(it), next(it), next(it)
    gq_ref, gk_ref, gqd_ref, gkd_ref = next(it), next(it), next(it), next(it)
    g64_ref, g32_ref = next(it), next(it)
    if rope:
        c64, su64, sd64, c32, su32, sd32 = (next(it) for _ in range(6))
    qg_ref, kk_ref, vog_ref, qd_ref, kd_ref, vod_ref, cb_ref, ccu_ref = (next(it) for _ in range(8))
    if own:
        kown_ref, vown_ref, kdown_ref, vdown_ref = (next(it) for _ in range(4))

    mods = mod_ref[0]
    sh1, sc1 = mods[0:1], mods[1:2]
    gain1 = n1g_ref[...] * (1.0 + sc1)
    ones = jnp.ones((PRE_ROWS, LANES), BF16)

    for r in range(x_ref.shape[1] // PRE_ROWS):
        rs = slice(r * PRE_ROWS, (r + 1) * PRE_ROWS)
        x = x_ref[0, rs]
        inv = lax.rsqrt(jnp.mean(x * x, axis=-1, keepdims=True) + NORM_EPS)
        h = (x * inv) * gain1 + sh1
        z = jnp.dot(h.astype(BF16), w_in_ref[...], preferred_element_type=F32)

        def rope64(y):
            if not rope:
                return y
            return _rope_chunk(y, c64[rs, :], su64[rs, :], sd64[rs, :], HEAD_DIM // 4)

        def rope32(y):
            if not rope:
                return y
            return _rope_chunk(y, c32[rs, :], su32[rs, :], sd32[rs, :], DIFF_QK_DIM // 4)

        for c in range(2):
            zc = z[:, OFF_QG + 256 * c: OFF_QG + 256 * (c + 1)]
            y = zc * _group_inv_rms(zc, g64_ref, HEAD_DIM)
            for j in range(2):
                yj = rope64(y[:, LANES * j: LANES * (j + 1)] * gq_ref[...])
                qg_ref[0, rs, 256 * c + LANES * j: 256 * c + LANES * (j + 1)] = yj.astype(BF16)

        zk = z[:, OFF_KG: OFF_KG + LANES]
        zv = z[:, OFF_VG: OFF_VG + LANES]
        kn = zk * _group_inv_rms(zk, g64_ref, HEAD_DIM) * gk_ref[...]
        if own:
            kown_ref[0, rs] = kn
            vown_ref[0, rs] = zv
        k0, k1 = _dup_halves(rope64(kn))
        kk_ref[0, 0, rs] = k0.astype(BF16)
        kk_ref[0, 1, rs] = k1.astype(BF16)
        v0, v1 = _dup_halves(zv)
        vog_ref[0, 0, rs, :LANES] = v0.astype(BF16)
        vog_ref[0, 1, rs, :LANES] = v1.astype(BF16)
        vog_ref[0, 0, rs, LANES:] = ones
        vog_ref[0, 1, rs, LANES:] = ones

        cb_ref[0, rs] = z[:, OFF_CB: OFF_CB + CONV_WIDTH]
        ccu_ref[0, rs] = z[:, OFF_CC: OFF_CC + CONV_WIDTH] * z[:, OFF_CU: OFF_CU + CONV_WIDTH]

        zq = z[:, OFF_QD: OFF_QD + 256]
        yq = zq * _group_inv_rms(zq, g32_ref, DIFF_QK_DIM)
        zkd = z[:, OFF_KD: OFF_KD + 256]
        ykd = zkd * _group_inv_rms(zkd, g32_ref, DIFF_QK_DIM)
        zvd = z[:, OFF_VD: OFF_VD + 256]
        for j in range(2):
            sl = slice(LANES * j, LANES * (j + 1))
            qd_ref[0, rs, sl] = rope32(yq[:, sl] * gqd_ref[...]).astype(BF16)
            kdj = ykd[:, sl] * gkd_ref[...]
            if own:
                kdown_ref[0, rs, sl] = kdj
            kd_ref[0, rs, sl] = rope32(kdj).astype(BF16)
            vod_ref[0, j, rs, :LANES] = zvd[:, sl].astype(BF16)
            vod_ref[0, j, rs, LANES:] = ones
        if own:
            vdown_ref[0, rs] = zvd


def _pre_call(x, layer, mods, mod_row, n1g, w_in, gq, gk, gqd, gkd, g64, g32, rope_tabs, own, tm):
    b, t, d = x.shape
    nt = t // tm
    rope = rope_tabs is not None
    gain_spec = _layer_spec((1, LANES), layer)
    in_specs = [
        pl.BlockSpec((1, tm, d), lambda i, j: (i, j, 0)),
        _mods_spec(layer, mod_row),
        _layer_spec((1, d), layer),
        _layer_spec((d, IN_WIDTH), layer),
        gain_spec, gain_spec, gain_spec, gain_spec,
        _const_spec((256, 256)), _const_spec((256, 256)),
    ]
    args = [x, mods, n1g, w_in, gq, gk, gqd, gkd, g64, g32]
    if rope:
        in_specs += [pl.BlockSpec((tm, LANES), lambda i, j: (j, 0))] * 6
        args += list(rope_tabs)
    tok = lambda w, dt: jax.ShapeDtypeStruct((b, t, w), dt)
    tok_spec = lambda w: pl.BlockSpec((1, tm, w), lambda i, j: (i, j, 0))
    pair = lambda w: jax.ShapeDtypeStruct((b, 2, t, w), BF16)
    pair_spec = lambda w: pl.BlockSpec((1, 2, tm, w), lambda i, j: (i, 0, j, 0))
    out_shape = [tok(GQA_WIDTH, BF16), pair(LANES), pair(2 * LANES), tok(256, BF16), tok(256, BF16),
                 pair(2 * LANES), tok(CONV_WIDTH, F32), tok(CONV_WIDTH, F32)]
    out_specs = [tok_spec(GQA_WIDTH), pair_spec(LANES), pair_spec(2 * LANES), tok_spec(256), tok_spec(256),
                 pair_spec(2 * LANES), tok_spec(CONV_WIDTH), tok_spec(CONV_WIDTH)]
    if own:
        out_shape += [tok(LANES, F32), tok(LANES, F32), tok(256, F32), tok(256, F32)]
        out_specs += [tok_spec(LANES), tok_spec(LANES), tok_spec(256), tok_spec(256)]
    return pl.pallas_call(
        functools.partial(_pre_kernel, rope=rope, own=own),
        out_shape=tuple(out_shape),
        grid=(b, nt),
        in_specs=in_specs,
        out_specs=tuple(out_specs),
        compiler_params=_cparams(),
        name="pre_attention",
    )(*args)


def _key_tiles(ref, lead, n_rows):
    def load(t):
        return ref[lead + (pl.ds(t * KEY_TILE, KEY_TILE), slice(None))]
    return [functools.partial(load, t) for t in range(n_rows // KEY_TILE)]


def _row_norm2_max(x):
    xf = x.astype(F32)
    return jnp.max(jnp.sum(xf * xf, axis=-1, keepdims=True))


def _dot_nt(a, b):
    return lax.dot_general(a, b, (((1,), (1,)), ((), ())), preferred_element_type=F32)


def _attend(lhs, k_tiles, v_tiles, qb2_ref, layer, safe_ref, r_ref, lhs_ref, first_q_tile):
    @pl.when(first_q_tile)
    def _():
        kmax2 = functools.reduce(jnp.maximum, [_row_norm2_max(k()) for k in k_tiles])
        safe_ref[0] = (qb2_ref[layer] * kmax2 <= SAFE_EXPONENT ** 2).astype(jnp.int32)

    safe = safe_ref[0] == 1

    @pl.when(safe)
    def _():
        acc = None
        for k, v in zip(k_tiles, v_tiles):
            p = jnp.exp2(_dot_nt(lhs, k())).astype(BF16)
            part = jnp.dot(p, v(), preferred_element_type=F32)
            acc = part if acc is None else acc + part
        r_ref[...] = acc[:, :LANES] / acc[:, LANES:]

    @pl.when(jnp.logical_not(safe))
    def _():
        lhs_ref[...] = lhs

        def row_block(rb, carry):
            rows = pl.ds(pl.multiple_of(rb * EXACT_ROWS, EXACT_ROWS), EXACT_ROWS)
            lhs_b = lhs_ref[rows, :]
            scores = [_dot_nt(lhs_b, k()) for k in k_tiles]
            m = functools.reduce(jnp.maximum, [jnp.max(s, axis=-1, keepdims=True) for s in scores])
            acc = None
            for s, v in zip(scores, v_tiles):
                p = jnp.exp2(s - m).astype(BF16)
                part = jnp.dot(p, v(), preferred_element_type=F32)
                acc = part if acc is None else acc + part
            r_ref[rows, :] = acc[:, :LANES] / acc[:, LANES:]
            return carry

        lax.fori_loop(0, lhs.shape[0] // EXACT_ROWS, row_block, 0)


def _attn_scratch(tq):
    return [pltpu.VMEM((4 * tq, LANES), F32), pltpu.SMEM((1,), jnp.int32), pltpu.VMEM((4 * tq, LANES), BF16)]


def _query_norm_bound(gain, dim):
    return dim * 1.02 * jnp.max(gain * gain, axis=(1, 2))


_SMEM_SPEC = pl.BlockSpec(memory_space=pltpu.SMEM)


def _attn_cparams():
    return pltpu.CompilerParams(vmem_limit_bytes=VMEM_LIMIT,
                                dimension_semantics=("arbitrary", "arbitrary", "arbitrary"))


def _gqa_kernel(*refs, has_cache, layer):
    if has_cache:
        q_ref, kk_ref, vo_ref, kkc_ref, voc_ref, qb2_ref, o_ref, r_ref, safe_ref, lhs_ref = refs
        k_tiles = _key_tiles(kk_ref, (0, 0), kk_ref.shape[2]) + _key_tiles(kkc_ref, (0, 0, 0), kkc_ref.shape[3])
        v_tiles = _key_tiles(vo_ref, (0, 0), vo_ref.shape[2]) + _key_tiles(voc_ref, (0, 0, 0), voc_ref.shape[3])
    else:
        q_ref, kk_ref, vo_ref, qb2_ref, o_ref, r_ref, safe_ref, lhs_ref = refs
        k_tiles = _key_tiles(kk_ref, (0, 0), kk_ref.shape[2])
        v_tiles = _key_tiles(vo_ref, (0, 0), vo_ref.shape[2])
    q = q_ref[0]
    tq = q.shape[0]
    lo = lax.broadcasted_iota(jnp.int32, (tq, LANES), 1) < HEAD_DIM
    zero = jnp.zeros((tq, LANES), BF16)
    qa, qb = q[:, :LANES], q[:, LANES:]
    lhs = jnp.concatenate([jnp.where(lo, qa, zero), jnp.where(lo, zero, qa),
                           jnp.where(lo, qb, zero), jnp.where(lo, zero, qb)], axis=0)
    _attend(lhs, k_tiles, v_tiles, qb2_ref, layer, safe_ref, r_ref, lhs_ref, pl.program_id(2) == 0)
    oa = jnp.where(lo, r_ref[0:tq], r_ref[tq:2 * tq])
    ob = jnp.where(lo, r_ref[2 * tq:3 * tq], r_ref[3 * tq:4 * tq])
    o_ref[0, :, :LANES] = oa.astype(BF16)
    o_ref[0, :, LANES:] = ob.astype(BF16)


def _gqa_call(qg, kk, vog, cache, layer, qb2, tq):
    b, t, _ = qg.shape
    nq = t // tq
    in_specs = [
        pl.BlockSpec((1, tq, 256), lambda i, h, j: (i, j, h)),
        pl.BlockSpec((1, 1, t, LANES), lambda i, h, j: (i, h, 0, 0)),
        pl.BlockSpec((1, 1, t, 2 * LANES), lambda i, h, j: (i, h, 0, 0)),
    ]
    args = [qg, kk, vog]
    if cache is not None:
        kkc, voc = cache
        s = kkc.shape[3]
        in_specs += [
            pl.BlockSpec((1, 1, 1, s, LANES), lambda i, h, j: (i, layer, h, 0, 0)),
            pl.BlockSpec((1, 1, 1, s, 2 * LANES), lambda i, h, j: (i, layer, h, 0, 0)),
        ]
        args += [kkc, voc]
    in_specs.append(_SMEM_SPEC)
    args.append(qb2)
    return pl.pallas_call(
        functools.partial(_gqa_kernel, has_cache=cache is not None, layer=layer),
        out_shape=jax.ShapeDtypeStruct((b, t, GQA_WIDTH), BF16),
        grid=(b, GQA_KV_HEADS, nq),
        in_specs=in_specs,
        out_specs=pl.BlockSpec((1, tq, 256), lambda i, h, j: (i, j, h)),
        scratch_shapes=_attn_scratch(tq),
        compiler_params=_attn_cparams(),
        name="gqa_attention",
    )(*args)


def _diff_kernel(*refs, has_cache, layer, lam_init):
    if has_cache:
        q_ref, k_ref, vo_ref, kc_ref, voc_ref, lam_ref, sg_ref, qb2_ref, o_ref, r_ref, safe_ref, lhs_ref = refs
        k_tiles = _key_tiles(k_ref, (0,), k_ref.shape[1]) + _key_tiles(kc_ref, (0, 0), kc_ref.shape[2])
        v_tiles = _key_tiles(vo_ref, (0, 0), vo_ref.shape[2]) + _key_tiles(voc_ref, (0, 0, 0), voc_ref.shape[3])
    else:
        q_ref, k_ref, vo_ref, lam_ref, sg_ref, qb2_ref, o_ref, r_ref, safe_ref, lhs_ref = refs
        k_tiles = _key_tiles(k_ref, (0,), k_ref.shape[1])
        v_tiles = _key_tiles(vo_ref, (0, 0), vo_ref.shape[2])
    q = q_ref[0]
    tq = q.shape[0]
    lane = lax.broadcasted_iota(jnp.int32, (tq, LANES), 1)
    zero = jnp.zeros((tq, LANES), BF16)
    lhs = jnp.concatenate(
        [jnp.where((lane >= DIFF_QK_DIM * u) & (lane < DIFF_QK_DIM * (u + 1)), q, zero) for u in range(4)],
        axis=0)
    _attend(lhs, k_tiles, v_tiles, qb2_ref, layer, safe_ref, r_ref, lhs_ref, pl.program_id(2) == 0)
    lf = lam_ref[...]
    lam = (jnp.exp(jnp.sum(lf[0:1] * lf[1:2], axis=-1, keepdims=True))
           - jnp.exp(jnp.sum(lf[2:3] * lf[3:4], axis=-1, keepdims=True)) + lam_init)
    lo = lane < DIFF_V_DIM
    o = jnp.where(lo, r_ref[0:tq] - lam * r_ref[tq:2 * tq], r_ref[2 * tq:3 * tq] - lam * r_ref[3 * tq:4 * tq])
    sq = o * o
    ms_lo = jnp.sum(jnp.where(lo, sq, 0.0), axis=-1, keepdims=True) * (1.0 / DIFF_V_DIM)
    ms_hi = jnp.sum(jnp.where(lo, 0.0, sq), axis=-1, keepdims=True) * (1.0 / DIFF_V_DIM)
    inv = jnp.where(lo, lax.rsqrt(ms_lo + NORM_EPS), lax.rsqrt(ms_hi + NORM_EPS))
    o_ref[0] = ((o * inv) * sg_ref[...] * (1.0 - lam_init)).astype(BF16)


def _diff_call(qd, kd, vod, cache, layer, lam_p, subg, qb2, lam_init, tq):
    b, t, _ = qd.shape
    nq = t // tq
    in_specs = [
        pl.BlockSpec((1, tq, LANES), lambda i, h, j: (i, j, h)),
        pl.BlockSpec((1, t, LANES), lambda i, h, j: (i, 0, h)),
        pl.BlockSpec((1, 1, t, 2 * LANES), lambda i, h, j: (i, h, 0, 0)),
    ]
    args = [qd, kd, vod]
    if cache is not None:
        kdc, vodc = cache
        s = kdc.shape[2]
        in_specs += [
            pl.BlockSpec((1, 1, s, LANES), lambda i, h, j: (i, layer, 0, h)),
            pl.BlockSpec((1, 1, 1, s, 2 * LANES), lambda i, h, j: (i, layer, h, 0, 0)),
        ]
        args += [kdc, vodc]
    in_specs += [_layer_spec((4, DIFF_QK_DIM), layer), _layer_spec((1, LANES), layer), _SMEM_SPEC]
    args += [lam_p, subg, qb2]
    return pl.pallas_call(
        functools.partial(_diff_kernel, has_cache=cache is not None, layer=layer, lam_init=lam_init),
        out_shape=jax.ShapeDtypeStruct((b, t, DIFF_WIDTH), BF16),
        grid=(b, 2, nq),
        in_specs=in_specs,
        out_specs=pl.BlockSpec((1, tq, LANES), lambda i, h, j: (i, j, h)),
        scratch_shapes=_attn_scratch(tq),
        compiler_params=_attn_cparams(),
        name="diff_attention",
    )(*args)


def _post_kernel(*refs, halo, tm, t_len):
    it = iter(refs)
    n_blk = 3 if halo else 1

    def take():
        blocks = [next(it) for _ in range(n_blk)]
        if halo:
            return jnp.concatenate([blocks[0][0], blocks[1][0], blocks[2][0]], axis=0)
        return blocks[0][0]

    xe, og, od, cb, ccu = take(), take(), take(), take(), take()
    mod_ref, cw_ref, cbias_ref, wout_ref, n2g_ref = (next(it) for _ in range(5))
    up_ref, fcw_ref, fcb_ref, down_ref = (next(it) for _ in range(4))
    out_ref, ccu_scr, a_scr = next(it), next(it), next(it)

    h0 = HALO if halo else 0
    rows = tm + 2 * h0
    mods = mod_ref[0]
    g1, sh2, sc2, g2 = mods[2:3], mods[3:4], mods[4:5], mods[5:6]

    if halo:
        pos = pl.program_id(1) * tm - h0 + lax.broadcasted_iota(jnp.int32, (rows, 1), 0)
        valid = (pos >= 0) & (pos < t_len)
        ccu = jnp.where(valid, ccu, 0.0)

    ccu_scr[0:PAD] = jnp.zeros((PAD, CONV_WIDTH), F32)
    ccu_scr[PAD + rows: 2 * PAD + rows] = jnp.zeros((PAD, CONV_WIDTH), F32)
    ccu_scr[PAD: PAD + rows] = ccu
    cw = cw_ref[...]
    conv = (ccu_scr[PAD - 1: PAD - 1 + rows] * cw[0:1] + ccu * cw[1:2]
            + ccu_scr[PAD + 1: PAD + 1 + rows] * cw[2:3] + cbias_ref[...])
    oc = (cb * conv).astype(BF16)

    y = (jnp.dot(og, wout_ref[0:GQA_WIDTH], preferred_element_type=F32)
         + jnp.dot(oc, wout_ref[GQA_WIDTH: GQA_WIDTH + CONV_WIDTH], preferred_element_type=F32)
         + jnp.dot(od, wout_ref[GQA_WIDTH + CONV_WIDTH:], preferred_element_type=F32))
    xmid = xe + g1 * y

    inv = lax.rsqrt(jnp.mean(xmid * xmid, axis=-1, keepdims=True) + NORM_EPS)
    h2 = ((xmid * inv) * n2g_ref[...] * (1.0 + sc2) + sh2).astype(BF16)
    a = jnp.dot(h2, up_ref[:, :D_FF], preferred_element_type=F32)
    if halo:
        a = jnp.where(valid, a, 0.0)
    a_scr[0:PAD] = jnp.zeros((PAD, D_FF), F32)
    a_scr[PAD + rows: 2 * PAD + rows] = jnp.zeros((PAD, D_FF), F32)
    a_scr[PAD: PAD + rows] = a
    base = PAD + h0
    fcw = fcw_ref[...]
    ac = (a_scr[base - 1: base - 1 + tm] * fcw[0:1] + a_scr[base: base + tm] * fcw[1:2]
          + a_scr[base + 1: base + 1 + tm] * fcw[2:3] + fcb_ref[...])
    u = jnp.dot(h2[h0: h0 + tm], up_ref[:, D_FF:], preferred_element_type=F32)
    f = (ac * jax.nn.sigmoid(ac) * u).astype(BF16)
    out_ref[0] = xmid[h0: h0 + tm] + g2 * jnp.dot(f, down_ref[...], preferred_element_type=F32)


def _post_call(x, og, od, cb, ccu, layer, mods, mod_row, cw, cbias, w_out, n2g, up, fcw, fcb, down, tm):
    b, t, d = x.shape
    nt = t // tm
    halo = nt > 1
    per = tm // HALO
    last = t // HALO - 1

    def specs(width):
        main = pl.BlockSpec((1, tm, width), lambda i, j: (i, j, 0))
        if not halo:
            return [main]
        prev = pl.BlockSpec((1, HALO, width), lambda i, j: (i, jnp.maximum(j * per - 1, 0), 0))
        nxt = pl.BlockSpec((1, HALO, width), lambda i, j: (i, jnp.minimum((j + 1) * per, last), 0))
        return [prev, main, nxt]

    in_specs, args = [], []
    for arr in (x, og, od, cb, ccu):
        sp = specs(arr.shape[-1])
        in_specs += sp
        args += [arr] * len(sp)
    in_specs += [
        _mods_spec(layer, mod_row),
        _layer_spec((3, CONV_WIDTH), layer), _layer_spec((1, CONV_WIDTH), layer),
        _layer_spec((d, d), layer), _layer_spec((1, d), layer),
        _layer_spec((d, 2 * D_FF), layer), _layer_spec((3, D_FF), layer), _layer_spec((1, D_FF), layer),
        _layer_spec((D_FF, d), layer),
    ]
    args += [mods, cw, cbias, w_out, n2g, up, fcw, fcb, down]
    rows = tm + (2 * HALO if halo else 0)
    return pl.pallas_call(
        functools.partial(_post_kernel, halo=halo, tm=tm, t_len=t),
        out_shape=jax.ShapeDtypeStruct((b, t, d), F32),
        grid=(b, nt),
        in_specs=in_specs,
        out_specs=pl.BlockSpec((1, tm, d), lambda i, j: (i, j, 0)),
        scratch_shapes=[pltpu.VMEM((rows + 2 * PAD, CONV_WIDTH), F32),
                        pltpu.VMEM((rows + 2 * PAD, D_FF), F32)],
        compiler_params=_cparams(),
        name="post_attention",
    )(*args)


def _rope_tables(n_rows, dim):
    row = jnp.repeat(jnp.arange(n_rows), GRID_W).astype(F32)
    col = jnp.tile(jnp.arange(GRID_W), n_rows).astype(F32)
    nf = dim // 4
    freqs = ROPE_THETA ** (-jnp.arange(nf, dtype=F32) / nf)
    ar = row[:, None] * freqs[None, :]
    ac = col[:, None] * freqs[None, :]
    ang = jnp.concatenate([ar, ar, ac, ac], axis=-1)
    cos, sin = jnp.cos(ang), jnp.sin(ang)
    first = ((jnp.arange(dim) // nf) % 2 == 0)[None, :]
    sin_up = jnp.where(first, -sin, 0.0)
    sin_dn = jnp.where(first, 0.0, sin)
    rep = LANES // dim
    return tuple(jnp.tile(tab, (1, rep)) for tab in (cos, sin_up, sin_dn))


def _group_matrix(group):
    idx = jnp.arange(256) // group
    return (idx[:, None] == idx[None, :]).astype(BF16)


def kernel(x_prompt, x_sample, cache_gqa_k, cache_gqa_v, cache_diff_k, cache_diff_v, c, c_ctx,
           w_mod, b_mod, norm1_g, w_in, gqa_qn_g, gqa_kn_g, conv_w, conv_b, diff_qn_g, diff_kn_g,
           diff_lambda, diff_subln_g, w_out, norm2_g, ffn_up, ffn_conv_w, ffn_conv_b, ffn_down):
    depth = w_in.shape[0]
    batch, seq, _ = x_prompt.shape
    dec_batch, dec_seq, _ = x_sample.shape

    cond_all = jnp.zeros((MOD_ROWS, D_MODEL), F32).at[0].set(c_ctx).at[1:1 + dec_batch].set(c)
    mods = _mods_call(cond_all, w_mod, b_mod)
    kkc, vogc, kdc, vodc = _cache_prep_call(cache_gqa_k, cache_gqa_v, cache_diff_k, cache_diff_v)

    w_in_b = w_in.astype(BF16)
    w_out_b = w_out.astype(BF16)
    up_b = ffn_up.astype(BF16)
    down_b = ffn_down.astype(BF16)
    g64, g32 = _group_matrix(HEAD_DIM), _group_matrix(DIFF_QK_DIM)
    rope_tabs = _rope_tables(dec_seq // GRID_W, HEAD_DIM) + _rope_tables(dec_seq // GRID_W, DIFF_QK_DIM)

    row = lambda p: p.reshape(depth, 1, -1)
    lane_tiled = lambda p, rep: jnp.tile(p, (1, rep)).reshape(depth, 1, LANES)
    gq = lane_tiled(gqa_qn_g * (HEAD_DIM ** -0.5 * LOG2E), 2)
    gk = lane_tiled(gqa_kn_g, 2)
    gqd = lane_tiled(diff_qn_g * (DIFF_QK_DIM ** -0.5 * LOG2E), 4)
    gkd = lane_tiled(diff_kn_g, 4)
    subg = lane_tiled(diff_subln_g, 2)
    qb2_g = _query_norm_bound(gq, HEAD_DIM)
    qb2_d = _query_norm_bound(gqd, DIFF_QK_DIM)
    n1g, n2g, cbias, fcb = row(norm1_g), row(norm2_g), row(conv_b), row(ffn_conv_b)

    def layer(x, l, mod_row, rope, cache_g, cache_d, own, tm, tq):
        lam_init = 0.8 - 0.6 * math.exp(-0.3 * l)
        pre = _pre_call(x, l, mods, mod_row, n1g, w_in_b, gq, gk, gqd, gkd, g64, g32, rope, own, tm)
        qg, kk, vog, qd, kd, vod, cb, ccu = pre[:8]
        og = _gqa_call(qg, kk, vog, cache_g, l, qb2_g, tq)
        od = _diff_call(qd, kd, vod, cache_d, l, diff_lambda, subg, qb2_d, lam_init, tq)
        x = _post_call(x, og, od, cb, ccu, l, mods, mod_row, conv_w, cbias, w_out_b, n2g, up_b,
                       ffn_conv_w, fcb, down_b, tm)
        return x, pre[8:]

    xp = x_prompt
    owns = []
    for l in range(depth):
        xp, own = layer(xp, l, lambda i: 0, None, None, None, True, seq, seq)
        owns.append(own)
    new_gqa_k = jnp.stack([o[0] for o in owns], axis=1).reshape(batch, depth, seq, GQA_KV_HEADS, HEAD_DIM)
    new_gqa_v = jnp.stack([o[1] for o in owns], axis=1).reshape(batch, depth, seq, GQA_KV_HEADS, HEAD_DIM)
    new_diff_k = jnp.stack([o[2] for o in owns], axis=1).reshape(batch, depth, seq, DIFF_HEADS, 2, DIFF_QK_DIM)
    new_diff_v = jnp.stack([o[3] for o in owns], axis=1).reshape(batch, depth, seq, DIFF_HEADS, DIFF_V_DIM)

    xs = x_sample
    for l in range(depth):
        xs, _ = layer(xs, l, lambda i: i + 1, rope_tabs, (kkc, vogc), (kdc, vodc), False, 512, 512)

    return (xp, xs, new_gqa_k, new_gqa_v, new_diff_k, new_diff_v)
```

```python
import functools
import math

import jax
import jax.numpy as jnp
from jax import lax
from jax.experimental import pallas as pl
from jax.experimental.pallas import tpu as pltpu

F32 = jnp.float32
BF16 = jnp.bfloat16

D_MODEL = 1024
GRID_W = 64
HEAD_DIM = 64
GQA_HEADS = 8
GQA_KV_HEADS = 2
GQA_WIDTH = GQA_HEADS * HEAD_DIM
CONV_WIDTH = 256
DIFF_HEADS = 4
DIFF_QK_DIM = 32
DIFF_V_DIM = 64
DIFF_WIDTH = DIFF_HEADS * DIFF_V_DIM
IN_WIDTH = 2304
D_FF = 2816
ROPE_THETA = 10000.0
NORM_EPS = 1e-6
N_MOD = 6
LANES = 128
MOD_ROWS = 16

OFF_QG, OFF_KG, OFF_VG = 0, 512, 640
OFF_CB, OFF_CC, OFF_CU = 768, 1024, 1280
OFF_QD, OFF_KD, OFF_VD = 1536, 1792, 2048

HALO = 16
PAD = 8

VMEM_LIMIT = 60 * 1024 * 1024

KEY_TILE = 256
SAFE_EXPONENT = 60.0
LOG2E = math.log2(math.e)
PRE_ROWS = 128
FAST_ROWS = 256
EXACT_ROWS = 512


def _cparams():
    return pltpu.CompilerParams(vmem_limit_bytes=VMEM_LIMIT)


def _const_spec(shape):
    nd = len(shape)
    return pl.BlockSpec(shape, lambda *_: (0,) * nd, pipeline_mode=pl.Buffered(1))


def _layer_spec(shape, layer):
    nd = len(shape)
    return pl.BlockSpec((None,) + tuple(shape), lambda *_: (layer,) + (0,) * nd, pipeline_mode=pl.Buffered(1))


def _mods_spec(layer, mod_row):
    return pl.BlockSpec((None, 1, N_MOD, D_MODEL), lambda i, j: (layer, mod_row(i), 0, 0))


def _mods_kernel(cond_ref, w_ref, b_ref, o_ref):
    cnd = cond_ref[...]
    act = cnd * jax.nn.sigmoid(cnd)
    o_ref[0] = jnp.dot(act, w_ref[0], preferred_element_type=F32,
                       precision=lax.Precision.HIGHEST) + b_ref[0]


def _mods_call(cond_all, w_mod, b_mod):
    depth = w_mod.shape[0]
    width = N_MOD * D_MODEL
    tn = 1024
    out = pl.pallas_call(
        _mods_kernel,
        out_shape=jax.ShapeDtypeStruct((depth, MOD_ROWS, width), F32),
        grid=(depth, width // tn),
        in_specs=[
            pl.BlockSpec((MOD_ROWS, D_MODEL), lambda l, j: (0, 0)),
            pl.BlockSpec((1, D_MODEL, tn), lambda l, j: (l, 0, j)),
            pl.BlockSpec((1, 1, tn), lambda l, j: (l, 0, j)),
        ],
        out_specs=pl.BlockSpec((1, MOD_ROWS, tn), lambda l, j: (l, 0, j)),
        compiler_params=_cparams(),
        name="adaln_mods",
    )(cond_all, w_mod, b_mod.reshape(depth, 1, width))
    return out.reshape(depth, MOD_ROWS, N_MOD, D_MODEL)


def _dup_halves(x128):
    lo = lax.broadcasted_iota(jnp.int32, x128.shape, 1) < HEAD_DIM
    swapped = pltpu.roll(x128, HEAD_DIM, 1)
    return jnp.where(lo, x128, swapped), jnp.where(lo, swapped, x128)


def _cache_prep_kernel(gk_ref, gv_ref, dk_ref, dv_ref, kk_ref, vog_ref, kd_ref, vod_ref):
    ones = jnp.ones((gk_ref.shape[2], LANES), BF16)
    k0, k1 = _dup_halves(gk_ref[0, 0])
    kk_ref[0, 0, 0] = k0.astype(BF16)
    kk_ref[0, 0, 1] = k1.astype(BF16)
    v0, v1 = _dup_halves(gv_ref[0, 0])
    vog_ref[0, 0, 0, :, :LANES] = v0.astype(BF16)
    vog_ref[0, 0, 1, :, :LANES] = v1.astype(BF16)
    vog_ref[0, 0, 0, :, LANES:] = ones
    vog_ref[0, 0, 1, :, LANES:] = ones
    kd_ref[0, 0] = dk_ref[0, 0].astype(BF16)
    dv = dv_ref[0, 0]
    vod_ref[0, 0, 0, :, :LANES] = dv[:, :LANES].astype(BF16)
    vod_ref[0, 0, 1, :, :LANES] = dv[:, LANES:].astype(BF16)
    vod_ref[0, 0, 0, :, LANES:] = ones
    vod_ref[0, 0, 1, :, LANES:] = ones


def _cache_prep_call(cache_gqa_k, cache_gqa_v, cache_diff_k, cache_diff_v):
    b, depth, s = cache_gqa_k.shape[:3]
    gk = cache_gqa_k.reshape(b, depth, s, LANES)
    gv = cache_gqa_v.reshape(b, depth, s, LANES)
    dk = cache_diff_k.reshape(b, depth, s, 2 * LANES)
    dv = cache_diff_v.reshape(b, depth, s, 2 * LANES)
    idx4 = lambda i, l: (i, l, 0, 0)
    idx5 = lambda i, l: (i, l, 0, 0, 0)
    return pl.pallas_call(
        _cache_prep_kernel,
        out_shape=(
            jax.ShapeDtypeStruct((b, depth, 2, s, LANES), BF16),
            jax.ShapeDtypeStruct((b, depth, 2, s, 2 * LANES), BF16),
            jax.ShapeDtypeStruct((b, depth, s, 2 * LANES), BF16),
            jax.ShapeDtypeStruct((b, depth, 2, s, 2 * LANES), BF16),
        ),
        grid=(b, depth),
        in_specs=[
            pl.BlockSpec((1, 1, s, LANES), idx4),
            pl.BlockSpec((1, 1, s, LANES), idx4),
            pl.BlockSpec((1, 1, s, 2 * LANES), idx4),
            pl.BlockSpec((1, 1, s, 2 * LANES), idx4),
        ],
        out_specs=(
            pl.BlockSpec((1, 1, 2, s, LANES), idx5),
            pl.BlockSpec((1, 1, 2, s, 2 * LANES), idx5),
            pl.BlockSpec((1, 1, s, 2 * LANES), idx4),
            pl.BlockSpec((1, 1, 2, s, 2 * LANES), idx5),
        ),
        compiler_params=_cparams(),
        name="cache_prep",
    )(gk, gv, dk, dv)


def _group_inv_rms(z, g_ref, group):
    sq = z * z
    hi = sq.astype(BF16)
    lo = (sq - hi.astype(F32)).astype(BF16)
    width = z.shape[1]
    gmat = g_ref[:width, :width]
    ssum = (jnp.dot(hi, gmat, preferred_element_type=F32)
            + jnp.dot(lo, gmat, preferred_element_type=F32))
    return lax.rsqrt(ssum * (1.0 / group) + NORM_EPS)


def _rope_chunk(y, cos, sin_up, sin_dn, quarter):
    up = pltpu.roll(y, LANES - quarter, 1)
    dn = pltpu.roll(y, quarter, 1)
    return y * cos + up * sin_up + dn * sin_dn


def _pre_kernel(*refs, rope, own):
    it = iter(refs)
    x_ref, mod_ref, n1g_ref, w_in_ref = next(it), next(it), next(it), next(it)
    gq_ref, gk_ref, gqd_ref, gkd_ref = next(it), next(it), next(it), next(it)
    g64_ref, g32_ref = next(it), next(it)
    if rope:
        c64, su64, sd64, c32, su32, sd32 = (next(it) for _ in range(6))
    qg_ref, kk_ref, vog_ref, qd_ref, kd_ref, vod_ref, cb_ref, ccu_ref = (next(it) for _ in range(8))
    if own:
        kown_ref, vown_ref, kdown_ref, vdown_ref = (next(it) for _ in range(4))

    mods = mod_ref[0]
    sh1, sc1 = mods[0:1], mods[1:2]
    gain1 = n1g_ref[...] * (1.0 + sc1)
    ones = jnp.ones((PRE_ROWS, LANES), BF16)

    for r in range(x_ref.shape[1] // PRE_ROWS):
        rs = slice(r * PRE_ROWS, (r + 1) * PRE_ROWS)
        x = x_ref[0, rs]
        inv = lax.rsqrt(jnp.mean(x * x, axis=-1, keepdims=True) + NORM_EPS)
        h = (x * inv) * gain1 + sh1
        z = jnp.dot(h.astype(BF16), w_in_ref[...], preferred_element_type=F32)

        def rope64(y):
            if not rope:
                return y
            return _rope_chunk(y, c64[rs, :], su64[rs, :], sd64[rs, :], HEAD_DIM // 4)

        def rope32(y):
            if not rope:
                return y
            return _rope_chunk(y, c32[rs, :], su32[rs, :], sd32[rs, :], DIFF_QK_DIM // 4)

        for c in range(2):
            zc = z[:, OFF_QG + 256 * c: OFF_QG + 256 * (c + 1)]
            y = zc * _group_inv_rms(zc, g64_ref, HEAD_DIM)
            for j in range(2):
                yj = rope64(y[:, LANES * j: LANES * (j + 1)] * gq_ref[...])
                qg_ref[0, rs, 256 * c + LANES * j: 256 * c + LANES * (j + 1)] = yj.astype(BF16)

        zk = z[:, OFF_KG: OFF_KG + LANES]
        zv = z[:, OFF_VG: OFF_VG + LANES]
        kn = zk * _group_inv_rms(zk, g64_ref, HEAD_DIM) * gk_ref[...]
        if own:
            kown_ref[0, rs] = kn
            vown_ref[0, rs] = zv
        k0, k1 = _dup_halves(rope64(kn))
        kk_ref[0, 0, rs] = k0.astype(BF16)
        kk_ref[0, 1, rs] = k1.astype(BF16)
        v0, v1 = _dup_halves(zv)
        vog_ref[0, 0, rs, :LANES] = v0.astype(BF16)
        vog_ref[0, 1, rs, :LANES] = v1.astype(BF16)
        vog_ref[0, 0, rs, LANES:] = ones
        vog_ref[0, 1, rs, LANES:] = ones

        cb_ref[0, rs] = z[:, OFF_CB: OFF_CB + CONV_WIDTH]
        ccu_ref[0, rs] = z[:, OFF_CC: OFF_CC + CONV_WIDTH] * z[:, OFF_CU: OFF_CU + CONV_WIDTH]

        zq = z[:, OFF_QD: OFF_QD + 256]
        yq = zq * _group_inv_rms(zq, g32_ref, DIFF_QK_DIM)
        zkd = z[:, OFF_KD: OFF_KD + 256]
        ykd = zkd * _group_inv_rms(zkd, g32_ref, DIFF_QK_DIM)
        zvd = z[:, OFF_VD: OFF_VD + 256]
        for j in range(2):
            sl = slice(LANES * j, LANES * (j + 1))
            qd_ref[0, rs, sl] = rope32(yq[:, sl] * gqd_ref[...]).astype(BF16)
            kdj = ykd[:, sl] * gkd_ref[...]
            if own:
                kdown_ref[0, rs, sl] = kdj
            kd_ref[0, rs, sl] = rope32(kdj).astype(BF16)
            vod_ref[0, j, rs, :LANES] = zvd[:, sl].astype(BF16)
            vod_ref[0, j, rs, LANES:] = ones
        if own:
            vdown_ref[0, rs] = zvd


def _pre_call(x, layer, mods, mod_row, n1g, w_in, gq, gk, gqd, gkd, g64, g32, rope_tabs, own, tm):
    b, t, d = x.shape
    nt = t // tm
    rope = rope_tabs is not None
    gain_spec = _layer_spec((1, LANES), layer)
    in_specs = [
        pl.BlockSpec((1, tm, d), lambda i, j: (i, j, 0)),
        _mods_spec(layer, mod_row),
        _layer_spec((1, d), layer),
        _layer_spec((d, IN_WIDTH), layer),
        gain_spec, gain_spec, gain_spec, gain_spec,
        _const_spec((256, 256)), _const_spec((256, 256)),
    ]
    args = [x, mods, n1g, w_in, gq, gk, gqd, gkd, g64, g32]
    if rope:
        in_specs += [pl.BlockSpec((tm, LANES), lambda i, j: (j, 0))] * 6
        args += list(rope_tabs)
    tok = lambda w, dt: jax.ShapeDtypeStruct((b, t, w), dt)
    tok_spec = lambda w: pl.BlockSpec((1, tm, w), lambda i, j: (i, j, 0))
    pair = lambda w: jax.ShapeDtypeStruct((b, 2, t, w), BF16)
    pair_spec = lambda w: pl.BlockSpec((1, 2, tm, w), lambda i, j: (i, 0, j, 0))
    out_shape = [tok(GQA_WIDTH, BF16), pair(LANES), pair(2 * LANES), tok(256, BF16), tok(256, BF16),
                 pair(2 * LANES), tok(CONV_WIDTH, F32), tok(CONV_WIDTH, F32)]
    out_specs = [tok_spec(GQA_WIDTH), pair_spec(LANES), pair_spec(2 * LANES), tok_spec(256), tok_spec(256),
                 pair_spec(2 * LANES), tok_spec(CONV_WIDTH), tok_spec(CONV_WIDTH)]
    if own:
        out_shape += [tok(LANES, F32), tok(LANES, F32), tok(256, F32), tok(256, F32)]
        out_specs += [tok_spec(LANES), tok_spec(LANES), tok_spec(256), tok_spec(256)]
    return pl.pallas_call(
        functools.partial(_pre_kernel, rope=rope, own=own),
        out_shape=tuple(out_shape),
        grid=(b, nt),
        in_specs=in_specs,
        out_specs=tuple(out_specs),
        compiler_params=_cparams(),
        name="pre_attention",
    )(*args)


def _key_tiles(ref, lead, n_rows):
    def load(t):
        return ref[lead + (pl.ds(t * KEY_TILE, KEY_TILE), slice(None))]
    return [functools.partial(load, t) for t in range(n_rows // KEY_TILE)]


def _row_norm2_max(x):
    xf = x.astype(F32)
    return jnp.max(jnp.sum(xf * xf, axis=-1, keepdims=True))


def _dot_nt(a, b):
    return lax.dot_general(a, b, (((1,), (1,)), ((), ())), preferred_element_type=F32)


def _attend(lhs, k_tiles, v_tiles, qb2_ref, layer, safe_ref, r_ref, lhs_ref, first_q_tile):
    @pl.when(first_q_tile)
    def _():
        kmax2 = functools.reduce(jnp.maximum, [_row_norm2_max(k()) for k in k_tiles])
        safe_ref[0] = (qb2_ref[layer] * kmax2 <= SAFE_EXPONENT ** 2).astype(jnp.int32)

    safe = safe_ref[0] == 1

    @pl.when(safe)
    def _():
        for rb in range(lhs.shape[0] // FAST_ROWS):
            rows = slice(rb * FAST_ROWS, (rb + 1) * FAST_ROWS)
            lhs_b = lhs[rows]
            acc = None
            for k, v in zip(k_tiles, v_tiles):
                p = jnp.exp2(_dot_nt(lhs_b, k())).astype(BF16)
                part = jnp.dot(p, v(), preferred_element_type=F32)
                acc = part if acc is None else acc + part
            r_ref[rows, :] = acc[:, :LANES] / acc[:, LANES:]

    @pl.when(jnp.logical_not(safe))
    def _():
        lhs_ref[...] = lhs

        def row_block(rb, carry):
            rows = pl.ds(pl.multiple_of(rb * EXACT_ROWS, EXACT_ROWS), EXACT_ROWS)
            lhs_b = lhs_ref[rows, :]
            scores = [_dot_nt(lhs_b, k()) for k in k_tiles]
            m = functools.reduce(jnp.maximum, [jnp.max(s, axis=-1, keepdims=True) for s in scores])
            acc = None
            for s, v in zip(scores, v_tiles):
                p = jnp.exp2(s - m).astype(BF16)
                part = jnp.dot(p, v(), preferred_element_type=F32)
                acc = part if acc is None else acc + part
            r_ref[rows, :] = acc[:, :LANES] / acc[:, LANES:]
            return carry

        lax.fori_loop(0, lhs.shape[0] // EXACT_ROWS, row_block, 0)


def _attn_scratch(tq):
    return [pltpu.VMEM((4 * tq, LANES), F32), pltpu.SMEM((1,), jnp.int32), pltpu.VMEM((4 * tq, LANES), BF16)]


def _query_norm_bound(gain, dim):
    return dim * 1.02 * jnp.max(gain * gain, axis=(1, 2))


_SMEM_SPEC = pl.BlockSpec(memory_space=pltpu.SMEM)


def _attn_cparams():
    return pltpu.CompilerParams(vmem_limit_bytes=VMEM_LIMIT,
                                dimension_semantics=("arbitrary", "arbitrary", "arbitrary"))


def _gqa_kernel(*refs, has_cache, layer):
    if has_cache:
        q_ref, kk_ref, vo_ref, kkc_ref, voc_ref, qb2_ref, o_ref, r_ref, safe_ref, lhs_ref = refs
        k_tiles = _key_tiles(kk_ref, (0, 0), kk_ref.shape[2]) + _key_tiles(kkc_ref, (0, 0, 0), kkc_ref.shape[3])
        v_tiles = _key_tiles(vo_ref, (0, 0), vo_ref.shape[2]) + _key_tiles(voc_ref, (0, 0, 0), voc_ref.shape[3])
    else:
        q_ref, kk_ref, vo_ref, qb2_ref, o_ref, r_ref, safe_ref, lhs_ref = refs
        k_tiles = _key_tiles(kk_ref, (0, 0), kk_ref.shape[2])
        v_tiles = _key_tiles(vo_ref, (0, 0), vo_ref.shape[2])
    q = q_ref[0]
    tq = q.shape[0]
    lo = lax.broadcasted_iota(jnp.int32, (tq, LANES), 1) < HEAD_DIM
    zero = jnp.zeros((tq, LANES), BF16)
    qa, qb = q[:, :LANES], q[:, LANES:]
    lhs = jnp.concatenate([jnp.where(lo, qa, zero), jnp.where(lo, zero, qa),
                           jnp.where(lo, qb, zero), jnp.where(lo, zero, qb)], axis=0)
    _attend(lhs, k_tiles, v_tiles, qb2_ref, layer, safe_ref, r_ref, lhs_ref, pl.program_id(2) == 0)
    oa = jnp.where(lo, r_ref[0:tq], r_ref[tq:2 * tq])
    ob = jnp.where(lo, r_ref[2 * tq:3 * tq], r_ref[3 * tq:4 * tq])
    o_ref[0, :, :LANES] = oa.astype(BF16)
    o_ref[0, :, LANES:] = ob.astype(BF16)


def _gqa_call(qg, kk, vog, cache, layer, qb2, tq):
    b, t, _ = qg.shape
    nq = t // tq
    in_specs = [
        pl.BlockSpec((1, tq, 256), lambda i, h, j: (i, j, h)),
        pl.BlockSpec((1, 1, t, LANES), lambda i, h, j: (i, h, 0, 0)),
        pl.BlockSpec((1, 1, t, 2 * LANES), lambda i, h, j: (i, h, 0, 0)),
    ]
    args = [qg, kk, vog]
    if cache is not None:
        kkc, voc = cache
        s = kkc.shape[3]
        in_specs += [
            pl.BlockSpec((1, 1, 1, s, LANES), lambda i, h, j: (i, layer, h, 0, 0)),
            pl.BlockSpec((1, 1, 1, s, 2 * LANES), lambda i, h, j: (i, layer, h, 0, 0)),
        ]
        args += [kkc, voc]
    in_specs.append(_SMEM_SPEC)
    args.append(qb2)
    return pl.pallas_call(
        functools.partial(_gqa_kernel, has_cache=cache is not None, layer=layer),
        out_shape=jax.ShapeDtypeStruct((b, t, GQA_WIDTH), BF16),
        grid=(b, GQA_KV_HEADS, nq),
        in_specs=in_specs,
        out_specs=pl.BlockSpec((1, tq, 256), lambda i, h, j: (i, j, h)),
        scratch_shapes=_attn_scratch(tq),
        compiler_params=_attn_cparams(),
        name="gqa_attention",
    )(*args)


def _diff_kernel(*refs, has_cache, layer, lam_init):
    if has_cache:
        q_ref, k_ref, vo_ref, kc_ref, voc_ref, lam_ref, sg_ref, qb2_ref, o_ref, r_ref, safe_ref, lhs_ref = refs
        k_tiles = _key_tiles(k_ref, (0,), k_ref.shape[1]) + _key_tiles(kc_ref, (0, 0), kc_ref.shape[2])
        v_tiles = _key_tiles(vo_ref, (0, 0), vo_ref.shape[2]) + _key_tiles(voc_ref, (0, 0, 0), voc_ref.shape[3])
    else:
        q_ref, k_ref, vo_ref, lam_ref, sg_ref, qb2_ref, o_ref, r_ref, safe_ref, lhs_ref = refs
        k_tiles = _key_tiles(k_ref, (0,), k_ref.shape[1])
        v_tiles = _key_tiles(vo_ref, (0, 0), vo_ref.shape[2])
    q = q_ref[0]
    tq = q.shape[0]
    lane = lax.broadcasted_iota(jnp.int32, (tq, LANES), 1)
    zero = jnp.zeros((tq, LANES), BF16)
    lhs = jnp.concatenate(
        [jnp.where((lane >= DIFF_QK_DIM * u) & (lane < DIFF_QK_DIM * (u + 1)), q, zero) for u in range(4)],
        axis=0)
    _attend(lhs, k_tiles, v_tiles, qb2_ref, layer, safe_ref, r_ref, lhs_ref, pl.program_id(2) == 0)
    lf = lam_ref[...]
    lam = (jnp.exp(jnp.sum(lf[0:1] * lf[1:2], axis=-1, keepdims=True))
           - jnp.exp(jnp.sum(lf[2:3] * lf[3:4], axis=-1, keepdims=True)) + lam_init)
    lo = lane < DIFF_V_DIM
    o = jnp.where(lo, r_ref[0:tq] - lam * r_ref[tq:2 * tq], r_ref[2 * tq:3 * tq] - lam * r_ref[3 * tq:4 * tq])
    sq = o * o
    ms_lo = jnp.sum(jnp.where(lo, sq, 0.0), axis=-1, keepdims=True) * (1.0 / DIFF_V_DIM)
    ms_hi = jnp.sum(jnp.where(lo, 0.0, sq), axis=-1, keepdims=True) * (1.0 / DIFF_V_DIM)
    inv = jnp.where(lo, lax.rsqrt(ms_lo + NORM_EPS), lax.rsqrt(ms_hi + NORM_EPS))
    o_ref[0] = ((o * inv) * sg_ref[...] * (1.0 - lam_init)).astype(BF16)


def _diff_call(qd, kd, vod, cache, layer, lam_p, subg, qb2, lam_init, tq):
    b, t, _ = qd.shape
    nq = t // tq
    in_specs = [
        pl.BlockSpec((1, tq, LANES), lambda i, h, j: (i, j, h)),
        pl.BlockSpec((1, t, LANES), lambda i, h, j: (i, 0, h)),
        pl.BlockSpec((1, 1, t, 2 * LANES), lambda i, h, j: (i, h, 0, 0)),
    ]
    args = [qd, kd, vod]
    if cache is not None:
        kdc, vodc = cache
        s = kdc.shape[2]
        in_specs += [
            pl.BlockSpec((1, 1, s, LANES), lambda i, h, j: (i, layer, 0, h)),
            pl.BlockSpec((1, 1, 1, s, 2 * LANES), lambda i, h, j: (i, layer, h, 0, 0)),
        ]
        args += [kdc, vodc]
    in_specs += [_layer_spec((4, DIFF_QK_DIM), layer), _layer_spec((1, LANES), layer), _SMEM_SPEC]
    args += [lam_p, subg, qb2]
    return pl.pallas_call(
        functools.partial(_diff_kernel, has_cache=cache is not None, layer=layer, lam_init=lam_init),
        out_shape=jax.ShapeDtypeStruct((b, t, DIFF_WIDTH), BF16),
        grid=(b, 2, nq),
        in_specs=in_specs,
        out_specs=pl.BlockSpec((1, tq, LANES), lambda i, h, j: (i, j, h)),
        scratch_shapes=_attn_scratch(tq),
        compiler_params=_attn_cparams(),
        name="diff_attention",
    )(*args)


def _post_kernel(*refs, halo, tm, t_len):
    it = iter(refs)
    n_blk = 3 if halo else 1

    def take():
        blocks = [next(it) for _ in range(n_blk)]
        if halo:
            return jnp.concatenate([blocks[0][0], blocks[1][0], blocks[2][0]], axis=0)
        return blocks[0][0]

    xe, og, od, cb, ccu = take(), take(), take(), take(), take()
    mod_ref, cw_ref, cbias_ref, wout_ref, n2g_ref = (next(it) for _ in range(5))
    up_ref, fcw_ref, fcb_ref, down_ref = (next(it) for _ in range(4))
    out_ref, ccu_scr, a_scr = next(it), next(it), next(it)

    h0 = HALO if halo else 0
    rows = tm + 2 * h0
    mods = mod_ref[0]
    g1, sh2, sc2, g2 = mods[2:3], mods[3:4], mods[4:5], mods[5:6]

    if halo:
        pos = pl.program_id(1) * tm - h0 + lax.broadcasted_iota(jnp.int32, (rows, 1), 0)
        valid = (pos >= 0) & (pos < t_len)
        ccu = jnp.where(valid, ccu, 0.0)

    ccu_scr[0:PAD] = jnp.zeros((PAD, CONV_WIDTH), F32)
    ccu_scr[PAD + rows: 2 * PAD + rows] = jnp.zeros((PAD, CONV_WIDTH), F32)
    ccu_scr[PAD: PAD + rows] = ccu
    cw = cw_ref[...]
    conv = (ccu_scr[PAD - 1: PAD - 1 + rows] * cw[0:1] + ccu * cw[1:2]
            + ccu_scr[PAD + 1: PAD + 1 + rows] * cw[2:3] + cbias_ref[...])
    oc = (cb * conv).astype(BF16)

    y = (jnp.dot(og, wout_ref[0:GQA_WIDTH], preferred_element_type=F32)
         + jnp.dot(oc, wout_ref[GQA_WIDTH: GQA_WIDTH + CONV_WIDTH], preferred_element_type=F32)
         + jnp.dot(od, wout_ref[GQA_WIDTH + CONV_WIDTH:], preferred_element_type=F32))
    xmid = xe + g1 * y

    inv = lax.rsqrt(jnp.mean(xmid * xmid, axis=-1, keepdims=True) + NORM_EPS)
    h2 = ((xmid * inv) * n2g_ref[...] * (1.0 + sc2) + sh2).astype(BF16)
    a = jnp.dot(h2, up_ref[:, :D_FF], preferred_element_type=F32)
    if halo:
        a = jnp.where(valid, a, 0.0)
    a_scr[0:PAD] = jnp.zeros((PAD, D_FF), F32)
    a_scr[PAD + rows: 2 * PAD + rows] = jnp.zeros((PAD, D_FF), F32)
    a_scr[PAD: PAD + rows] = a
    base = PAD + h0
    fcw = fcw_ref[...]
    ac = (a_scr[base - 1: base - 1 + tm] * fcw[0:1] + a_scr[base: base + tm] * fcw[1:2]
          + a_scr[base + 1: base + 1 + tm] * fcw[2:3] + fcb_ref[...])
    u = jnp.dot(h2[h0: h0 + tm], up_ref[:, D_FF:], preferred_element_type=F32)
    f = (ac * jax.nn.sigmoid(ac) * u).astype(BF16)
    out_ref[0] = xmid[h0: h0 + tm] + g2 * jnp.dot(f, down_ref[...], preferred_element_type=F32)


def _post_call(x, og, od, cb, ccu, layer, mods, mod_row, cw, cbias, w_out, n2g, up, fcw, fcb, down, tm):
    b, t, d = x.shape
    nt = t // tm
    halo = nt > 1
    per = tm // HALO
    last = t // HALO - 1

    def specs(width):
        main = pl.BlockSpec((1, tm, width), lambda i, j: (i, j, 0))
        if not halo:
            return [main]
        prev = pl.BlockSpec((1, HALO, width), lambda i, j: (i, jnp.maximum(j * per - 1, 0), 0))
        nxt = pl.BlockSpec((1, HALO, width), lambda i, j: (i, jnp.minimum((j + 1) * per, last), 0))
        return [prev, main, nxt]

    in_specs, args = [], []
    for arr in (x, og, od, cb, ccu):
        sp = specs(arr.shape[-1])
        in_specs += sp
        args += [arr] * len(sp)
    in_specs += [
        _mods_spec(layer, mod_row),
        _layer_spec((3, CONV_WIDTH), layer), _layer_spec((1, CONV_WIDTH), layer),
        _layer_spec((d, d), layer), _layer_spec((1, d), layer),
        _layer_spec((d, 2 * D_FF), layer), _layer_spec((3, D_FF), layer), _layer_spec((1, D_FF), layer),
        _layer_spec((D_FF, d), layer),
    ]
    args += [mods, cw, cbias, w_out, n2g, up, fcw, fcb, down]
    rows = tm + (2 * HALO if halo else 0)
    return pl.pallas_call(
        functools.partial(_post_kernel, halo=halo, tm=tm, t_len=t),
        out_shape=jax.ShapeDtypeStruct((b, t, d), F32),
        grid=(b, nt),
        in_specs=in_specs,
        out_specs=pl.BlockSpec((1, tm, d), lambda i, j: (i, j, 0)),
        scratch_shapes=[pltpu.VMEM((rows + 2 * PAD, CONV_WIDTH), F32),
                        pltpu.VMEM((rows + 2 * PAD, D_FF), F32)],
        compiler_params=_cparams(),
        name="post_attention",
    )(*args)


def _rope_tables(n_rows, dim):
    row = jnp.repeat(jnp.arange(n_rows), GRID_W).astype(F32)
    col = jnp.tile(jnp.arange(GRID_W), n_rows).astype(F32)
    nf = dim // 4
    freqs = ROPE_THETA ** (-jnp.arange(nf, dtype=F32) / nf)
    ar = row[:, None] * freqs[None, :]
    ac = col[:, None] * freqs[None, :]
    ang = jnp.concatenate([ar, ar, ac, ac], axis=-1)
    cos, sin = jnp.cos(ang), jnp.sin(ang)
    first = ((jnp.arange(dim) // nf) % 2 == 0)[None, :]
    sin_up = jnp.where(first, -sin, 0.0)
    sin_dn = jnp.where(first, 0.0, sin)
    rep = LANES // dim
    return tuple(jnp.tile(tab, (1, rep)) for tab in (cos, sin_up, sin_dn))


def _group_matrix(group):
    idx = jnp.arange(256) // group
    return (idx[:, None] == idx[None, :]).astype(BF16)


def kernel(x_prompt, x_sample, cache_gqa_k, cache_gqa_v, cache_diff_k, cache_diff_v, c, c_ctx,
           w_mod, b_mod, norm1_g, w_in, gqa_qn_g, gqa_kn_g, conv_w, conv_b, diff_qn_g, diff_kn_g,
           diff_lambda, diff_subln_g, w_out, norm2_g, ffn_up, ffn_conv_w, ffn_conv_b, ffn_down):
    depth = w_in.shape[0]
    batch, seq, _ = x_prompt.shape
    dec_batch, dec_seq, _ = x_sample.shape

    cond_all = jnp.zeros((MOD_ROWS, D_MODEL), F32).at[0].set(c_ctx).at[1:1 + dec_batch].set(c)
    mods = _mods_call(cond_all, w_mod, b_mod)
    kkc, vogc, kdc, vodc = _cache_prep_call(cache_gqa_k, cache_gqa_v, cache_diff_k, cache_diff_v)

    w_in_b = w_in.astype(BF16)
    w_out_b = w_out.astype(BF16)
    up_b = ffn_up.astype(BF16)
    down_b = ffn_down.astype(BF16)
    g64, g32 = _group_matrix(HEAD_DIM), _group_matrix(DIFF_QK_DIM)
    rope_tabs = _rope_tables(dec_seq // GRID_W, HEAD_DIM) + _rope_tables(dec_seq // GRID_W, DIFF_QK_DIM)

    row = lambda p: p.reshape(depth, 1, -1)
    lane_tiled = lambda p, rep: jnp.tile(p, (1, rep)).reshape(depth, 1, LANES)
    gq = lane_tiled(gqa_qn_g * (HEAD_DIM ** -0.5 * LOG2E), 2)
    gk = lane_tiled(gqa_kn_g, 2)
    gqd = lane_tiled(diff_qn_g * (DIFF_QK_DIM ** -0.5 * LOG2E), 4)
    gkd = lane_tiled(diff_kn_g, 4)
    subg = lane_tiled(diff_subln_g, 2)
    qb2_g = _query_norm_bound(gq, HEAD_DIM)
    qb2_d = _query_norm_bound(gqd, DIFF_QK_DIM)
    n1g, n2g, cbias, fcb = row(norm1_g), row(norm2_g), row(conv_b), row(ffn_conv_b)

    def layer(x, l, mod_row, rope, cache_g, cache_d, own, tm, tq):
        lam_init = 0.8 - 0.6 * math.exp(-0.3 * l)
        pre = _pre_call(x, l, mods, mod_row, n1g, w_in_b, gq, gk, gqd, gkd, g64, g32, rope, own, tm)
        qg, kk, vog, qd, kd, vod, cb, ccu = pre[:8]
        og = _gqa_call(qg, kk, vog, cache_g, l, qb2_g, tq)
        od = _diff_call(qd, kd, vod, cache_d, l, diff_lambda, subg, qb2_d, lam_init, tq)
        x = _post_call(x, og, od, cb, ccu, l, mods, mod_row, conv_w, cbias, w_out_b, n2g, up_b,
                       ffn_conv_w, fcb, down_b, tm)
        return x, pre[8:]

    xp = x_prompt
    owns = []
    for l in range(depth):
        xp, own = layer(xp, l, lambda i: 0, None, None, None, True, seq, seq)
        owns.append(own)
    new_gqa_k = jnp.stack([o[0] for o in owns], axis=1).reshape(batch, depth, seq, GQA_KV_HEADS, HEAD_DIM)
    new_gqa_v = jnp.stack([o[1] for o in owns], axis=1).reshape(batch, depth, seq, GQA_KV_HEADS, HEAD_DIM)
    new_diff_k = jnp.stack([o[2] for o in owns], axis=1).reshape(batch, depth, seq, DIFF_HEADS, 2, DIFF_QK_DIM)
    new_diff_v = jnp.stack([o[3] for o in owns], axis=1).reshape(batch, depth, seq, DIFF_HEADS, DIFF_V_DIM)

    xs = x_sample
    for l in range(depth):
        xs, _ = layer(xs, l, lambda i: i + 1, rope_tabs, (kkc, vogc), (kdc, vodc), False, 512, 512)

    return (xp, xs, new_gqa_k, new_gqa_v, new_diff_k, new_diff_v)
```

```python
import functools
import math

import jax
import jax.numpy as jnp
from jax import lax
from jax.experimental import pallas as pl
from jax.experimental.pallas import tpu as pltpu

F32 = jnp.float32
BF16 = jnp.bfloat16

D_MODEL = 1024
GRID_W = 64
HEAD_DIM = 64
GQA_HEADS = 8
GQA_KV_HEADS = 2
GQA_WIDTH = GQA_HEADS * HEAD_DIM
CONV_WIDTH = 256
DIFF_HEADS = 4
DIFF_QK_DIM = 32
DIFF_V_DIM = 64
DIFF_WIDTH = DIFF_HEADS * DIFF_V_DIM
IN_WIDTH = 2304
D_FF = 2816
ROPE_THETA = 10000.0
NORM_EPS = 1e-6
N_MOD = 6
LANES = 128
MOD_ROWS = 16

OFF_QG, OFF_KG, OFF_VG = 0, 512, 640
OFF_CB, OFF_CC, OFF_CU = 768, 1024, 1280
OFF_QD, OFF_KD, OFF_VD = 1536, 1792, 2048

HALO = 16
PAD = 8

VMEM_LIMIT = 60 * 1024 * 1024

KEY_TILE = 256
SAFE_EXPONENT = 60.0
LOG2E = math.log2(math.e)
PRE_ROWS = 128
QUERY_BLOCK = 64
EXACT_ROWS = 512


def _cparams():
    return pltpu.CompilerParams(vmem_limit_bytes=VMEM_LIMIT)


def _const_spec(shape):
    nd = len(shape)
    return pl.BlockSpec(shape, lambda *_: (0,) * nd, pipeline_mode=pl.Buffered(1))


def _layer_spec(shape, layer):
    nd = len(shape)
    return pl.BlockSpec((None,) + tuple(shape), lambda *_: (layer,) + (0,) * nd, pipeline_mode=pl.Buffered(1))


def _mods_spec(layer, mod_row):
    return pl.BlockSpec((None, 1, N_MOD, D_MODEL), lambda i, j: (layer, mod_row(i), 0, 0))


def _mods_kernel(cond_ref, w_ref, b_ref, o_ref):
    cnd = cond_ref[...]
    act = cnd * jax.nn.sigmoid(cnd)
    o_ref[0] = jnp.dot(act, w_ref[0], preferred_element_type=F32,
                       precision=lax.Precision.HIGHEST) + b_ref[0]


def _mods_call(cond_all, w_mod, b_mod):
    depth = w_mod.shape[0]
    width = N_MOD * D_MODEL
    tn = 1024
    out = pl.pallas_call(
        _mods_kernel,
        out_shape=jax.ShapeDtypeStruct((depth, MOD_ROWS, width), F32),
        grid=(depth, width // tn),
        in_specs=[
            pl.BlockSpec((MOD_ROWS, D_MODEL), lambda l, j: (0, 0)),
            pl.BlockSpec((1, D_MODEL, tn), lambda l, j: (l, 0, j)),
            pl.BlockSpec((1, 1, tn), lambda l, j: (l, 0, j)),
        ],
        out_specs=pl.BlockSpec((1, MOD_ROWS, tn), lambda l, j: (l, 0, j)),
        compiler_params=_cparams(),
        name="adaln_mods",
    )(cond_all, w_mod, b_mod.reshape(depth, 1, width))
    return out.reshape(depth, MOD_ROWS, N_MOD, D_MODEL)


def _dup_halves(x128):
    lo = lax.broadcasted_iota(jnp.int32, x128.shape, 1) < HEAD_DIM
    swapped = pltpu.roll(x128, HEAD_DIM, 1)
    return jnp.where(lo, x128, swapped), jnp.where(lo, swapped, x128)


def _cache_prep_kernel(gk_ref, gv_ref, dk_ref, dv_ref, kk_ref, vog_ref, kd_ref, vod_ref):
    ones = jnp.ones((gk_ref.shape[2], LANES), BF16)
    k0, k1 = _dup_halves(gk_ref[0, 0])
    kk_ref[0, 0, 0] = k0.astype(BF16)
    kk_ref[0, 0, 1] = k1.astype(BF16)
    v0, v1 = _dup_halves(gv_ref[0, 0])
    vog_ref[0, 0, 0, :, :LANES] = v0.astype(BF16)
    vog_ref[0, 0, 1, :, :LANES] = v1.astype(BF16)
    vog_ref[0, 0, 0, :, LANES:] = ones
    vog_ref[0, 0, 1, :, LANES:] = ones
    kd_ref[0, 0] = dk_ref[0, 0].astype(BF16)
    dv = dv_ref[0, 0]
    vod_ref[0, 0, 0, :, :LANES] = dv[:, :LANES].astype(BF16)
    vod_ref[0, 0, 1, :, :LANES] = dv[:, LANES:].astype(BF16)
    vod_ref[0, 0, 0, :, LANES:] = ones
    vod_ref[0, 0, 1, :, LANES:] = ones


def _cache_prep_call(cache_gqa_k, cache_gqa_v, cache_diff_k, cache_diff_v):
    b, depth, s = cache_gqa_k.shape[:3]
    gk = cache_gqa_k.reshape(b, depth, s, LANES)
    gv = cache_gqa_v.reshape(b, depth, s, LANES)
    dk = cache_diff_k.reshape(b, depth, s, 2 * LANES)
    dv = cache_diff_v.reshape(b, depth, s, 2 * LANES)
    idx4 = lambda i, l: (i, l, 0, 0)
    idx5 = lambda i, l: (i, l, 0, 0, 0)
    return pl.pallas_call(
        _cache_prep_kernel,
        out_shape=(
            jax.ShapeDtypeStruct((b, depth, 2, s, LANES), BF16),
            jax.ShapeDtypeStruct((b, depth, 2, s, 2 * LANES), BF16),
            jax.ShapeDtypeStruct((b, depth, s, 2 * LANES), BF16),
            jax.ShapeDtypeStruct((b, depth, 2, s, 2 * LANES), BF16),
        ),
        grid=(b, depth),
        in_specs=[
            pl.BlockSpec((1, 1, s, LANES), idx4),
            pl.BlockSpec((1, 1, s, LANES), idx4),
            pl.BlockSpec((1, 1, s, 2 * LANES), idx4),
            pl.BlockSpec((1, 1, s, 2 * LANES), idx4),
        ],
        out_specs=(
            pl.BlockSpec((1, 1, 2, s, LANES), idx5),
            pl.BlockSpec((1, 1, 2, s, 2 * LANES), idx5),
            pl.BlockSpec((1, 1, s, 2 * LANES), idx4),
            pl.BlockSpec((1, 1, 2, s, 2 * LANES), idx5),
        ),
        compiler_params=_cparams(),
        name="cache_prep",
    )(gk, gv, dk, dv)


def _group_inv_rms(z, g_ref, group):
    sq = z * z
    hi = sq.astype(BF16)
    lo = (sq - hi.astype(F32)).astype(BF16)
    width = z.shape[1]
    gmat = g_ref[:width, :width]
    ssum = (jnp.dot(hi, gmat, preferred_element_type=F32)
            + jnp.dot(lo, gmat, preferred_element_type=F32))
    return lax.rsqrt(ssum * (1.0 / group) + NORM_EPS)


def _rope_chunk(y, cos, sin_up, sin_dn, quarter):
    up = pltpu.roll(y, LANES - quarter, 1)
    dn = pltpu.roll(y, quarter, 1)
    return y * cos + up * sin_up + dn * sin_dn


def _pre_kernel(*refs, rope, own):
    it = iter(refs)
    x_ref, mod_ref, n1g_ref, w_in_ref = next(it), next(it), next(it), next(it)
    gq_ref, gk_ref, gqd_ref, gkd_ref = next(it), next(it), next(it), next(it)
    g64_ref, g32_ref = next(it), next(it)
    if rope:
        c64, su64, sd64, c32, su32, sd32 = (next(it) for _ in range(6))
    qg_ref, kk_ref, vog_ref, qd_ref, kd_ref, vod_ref, cb_ref, ccu_ref = (next(it) for _ in range(8))
    if own:
        kown_ref, vown_ref, kdown_ref, vdown_ref = (next(it) for _ in range(4))

    mods = mod_ref[0]
    sh1, sc1 = mods[0:1], mods[1:2]
    gain1 = n1g_ref[...] * (1.0 + sc1)
    ones = jnp.ones((PRE_ROWS, LANES), BF16)

    for r in range(x_ref.shape[1] // PRE_ROWS):
        rs = slice(r * PRE_ROWS, (r + 1) * PRE_ROWS)
        x = x_ref[0, rs]
        inv = lax.rsqrt(jnp.mean(x * x, axis=-1, keepdims=True) + NORM_EPS)
        h = (x * inv) * gain1 + sh1
        z = jnp.dot(h.astype(BF16), w_in_ref[...], preferred_element_type=F32)

        def rope64(y):
            if not rope:
                return y
            return _rope_chunk(y, c64[rs, :], su64[rs, :], sd64[rs, :], HEAD_DIM // 4)

        def rope32(y):
            if not rope:
                return y
            return _rope_chunk(y, c32[rs, :], su32[rs, :], sd32[rs, :], DIFF_QK_DIM // 4)

        for c in range(2):
            zc = z[:, OFF_QG + 256 * c: OFF_QG + 256 * (c + 1)]
            y = zc * _group_inv_rms(zc, g64_ref, HEAD_DIM)
            for j in range(2):
                yj = rope64(y[:, LANES * j: LANES * (j + 1)] * gq_ref[...])
                qg_ref[0, rs, 256 * c + LANES * j: 256 * c + LANES * (j + 1)] = yj.astype(BF16)

        zk = z[:, OFF_KG: OFF_KG + LANES]
        zv = z[:, OFF_VG: OFF_VG + LANES]
        kn = zk * _group_inv_rms(zk, g64_ref, HEAD_DIM) * gk_ref[...]
        if own:
            kown_ref[0, rs] = kn
            vown_ref[0, rs] = zv
        k0, k1 = _dup_halves(rope64(kn))
        kk_ref[0, 0, rs] = k0.astype(BF16)
        kk_ref[0, 1, rs] = k1.astype(BF16)
        v0, v1 = _dup_halves(zv)
        vog_ref[0, 0, rs, :LANES] = v0.astype(BF16)
        vog_ref[0, 1, rs, :LANES] = v1.astype(BF16)
        vog_ref[0, 0, rs, LANES:] = ones
        vog_ref[0, 1, rs, LANES:] = ones

        cb_ref[0, rs] = z[:, OFF_CB: OFF_CB + CONV_WIDTH]
        ccu_ref[0, rs] = z[:, OFF_CC: OFF_CC + CONV_WIDTH] * z[:, OFF_CU: OFF_CU + CONV_WIDTH]

        zq = z[:, OFF_QD: OFF_QD + 256]
        yq = zq * _group_inv_rms(zq, g32_ref, DIFF_QK_DIM)
        zkd = z[:, OFF_KD: OFF_KD + 256]
        ykd = zkd * _group_inv_rms(zkd, g32_ref, DIFF_QK_DIM)
        zvd = z[:, OFF_VD: OFF_VD + 256]
        for j in range(2):
            sl = slice(LANES * j, LANES * (j + 1))
            qd_ref[0, rs, sl] = rope32(yq[:, sl] * gqd_ref[...]).astype(BF16)
            kdj = ykd[:, sl] * gkd_ref[...]
            if own:
                kdown_ref[0, rs, sl] = kdj
            kd_ref[0, rs, sl] = rope32(kdj).astype(BF16)
            vod_ref[0, j, rs, :LANES] = zvd[:, sl].astype(BF16)
            vod_ref[0, j, rs, LANES:] = ones
        if own:
            vdown_ref[0, rs] = zvd


def _pre_call(x, layer, mods, mod_row, n1g, w_in, gq, gk, gqd, gkd, g64, g32, rope_tabs, own, tm):
    b, t, d = x.shape
    nt = t // tm
    rope = rope_tabs is not None
    gain_spec = _layer_spec((1, LANES), layer)
    in_specs = [
        pl.BlockSpec((1, tm, d), lambda i, j: (i, j, 0)),
        _mods_spec(layer, mod_row),
        _layer_spec((1, d), layer),
        _layer_spec((d, IN_WIDTH), layer),
        gain_spec, gain_spec, gain_spec, gain_spec,
        _const_spec((256, 256)), _const_spec((256, 256)),
    ]
    args = [x, mods, n1g, w_in, gq, gk, gqd, gkd, g64, g32]
    if rope:
        in_specs += [pl.BlockSpec((tm, LANES), lambda i, j: (j, 0))] * 6
        args += list(rope_tabs)
    tok = lambda w, dt: jax.ShapeDtypeStruct((b, t, w), dt)
    tok_spec = lambda w: pl.BlockSpec((1, tm, w), lambda i, j: (i, j, 0))
    pair = lambda w: jax.ShapeDtypeStruct((b, 2, t, w), BF16)
    pair_spec = lambda w: pl.BlockSpec((1, 2, tm, w), lambda i, j: (i, 0, j, 0))
    out_shape = [tok(GQA_WIDTH, BF16), pair(LANES), pair(2 * LANES), tok(256, BF16), tok(256, BF16),
                 pair(2 * LANES), tok(CONV_WIDTH, F32), tok(CONV_WIDTH, F32)]
    out_specs = [tok_spec(GQA_WIDTH), pair_spec(LANES), pair_spec(2 * LANES), tok_spec(256), tok_spec(256),
                 pair_spec(2 * LANES), tok_spec(CONV_WIDTH), tok_spec(CONV_WIDTH)]
    if own:
        out_shape += [tok(LANES, F32), tok(LANES, F32), tok(256, F32), tok(256, F32)]
        out_specs += [tok_spec(LANES), tok_spec(LANES), tok_spec(256), tok_spec(256)]
    return pl.pallas_call(
        functools.partial(_pre_kernel, rope=rope, own=own),
        out_shape=tuple(out_shape),
        grid=(b, nt),
        in_specs=in_specs,
        out_specs=tuple(out_specs),
        compiler_params=_cparams(),
        name="pre_attention",
    )(*args)


def _key_tiles(ref, lead, n_rows):
    def load(t):
        return ref[lead + (pl.ds(t * KEY_TILE, KEY_TILE), slice(None))]
    return [functools.partial(load, t) for t in range(n_rows // KEY_TILE)]


def _row_norm2_max(x):
    xf = x.astype(F32)
    return jnp.max(jnp.sum(xf * xf, axis=-1, keepdims=True))


def _dot_nt(a, b):
    return lax.dot_general(a, b, (((1,), (1,)), ((), ())), preferred_element_type=F32)


def _scores_are_safe(k_tiles, qb2_ref, layer, safe_ref, first_q_tile):
    @pl.when(first_q_tile)
    def _():
        kmax2 = functools.reduce(jnp.maximum, [_row_norm2_max(k()) for k in k_tiles])
        safe_ref[0] = (qb2_ref[layer] * kmax2 <= SAFE_EXPONENT ** 2).astype(jnp.int32)

    return safe_ref[0] == 1


def _softmax_pv_shifted(lhs_ref, k_tiles, v_tiles, r_ref):
    def row_block(rb, carry):
        rows = pl.ds(pl.multiple_of(rb * EXACT_ROWS, EXACT_ROWS), EXACT_ROWS)
        lhs_b = lhs_ref[rows, :]
        scores = [_dot_nt(lhs_b, k()) for k in k_tiles]
        m = functools.reduce(jnp.maximum, [jnp.max(s, axis=-1, keepdims=True) for s in scores])
        acc = None
        for s, v in zip(scores, v_tiles):
            p = jnp.exp2(s - m).astype(BF16)
            part = jnp.dot(p, v(), preferred_element_type=F32)
            acc = part if acc is None else acc + part
        r_ref[rows, :] = acc[:, :LANES] / acc[:, LANES:]
        return carry

    lax.fori_loop(0, lhs_ref.shape[0] // EXACT_ROWS, row_block, 0)


def _attention(tq, lhs_block, finish, k_tiles, v_tiles, qb2_ref, layer, safe_ref, r_ref, lhs_ref):
    n_blocks = tq // QUERY_BLOCK
    rows = 4 * QUERY_BLOCK
    safe = _scores_are_safe(k_tiles, qb2_ref, layer, safe_ref, pl.program_id(2) == 0)

    @pl.when(safe)
    def _():
        for qb in range(n_blocks):
            lhs_b = lhs_block(qb)
            acc = None
            for k, v in zip(k_tiles, v_tiles):
                p = jnp.exp2(_dot_nt(lhs_b, k())).astype(BF16)
                part = jnp.dot(p, v(), preferred_element_type=F32)
                acc = part if acc is None else acc + part
            finish(acc[:, :LANES] / acc[:, LANES:], qb)

    @pl.when(jnp.logical_not(safe))
    def _():
        for qb in range(n_blocks):
            lhs_ref[qb * rows:(qb + 1) * rows, :] = lhs_block(qb)
        _softmax_pv_shifted(lhs_ref, k_tiles, v_tiles, r_ref)
        for qb in range(n_blocks):
            finish(r_ref[qb * rows:(qb + 1) * rows, :], qb)


def _attn_scratch(tq):
    return [pltpu.VMEM((4 * tq, LANES), F32), pltpu.SMEM((1,), jnp.int32), pltpu.VMEM((4 * tq, LANES), BF16)]


def _query_norm_bound(gain, dim):
    return dim * 1.02 * jnp.max(gain * gain, axis=(1, 2))


_SMEM_SPEC = pl.BlockSpec(memory_space=pltpu.SMEM)


def _attn_cparams():
    return pltpu.CompilerParams(vmem_limit_bytes=VMEM_LIMIT,
                                dimension_semantics=("arbitrary", "arbitrary", "arbitrary"))


def _gqa_kernel(*refs, has_cache, layer):
    if has_cache:
        q_ref, kk_ref, vo_ref, kkc_ref, voc_ref, qb2_ref, o_ref, r_ref, safe_ref, lhs_ref = refs
        k_tiles = _key_tiles(kk_ref, (0, 0), kk_ref.shape[2]) + _key_tiles(kkc_ref, (0, 0, 0), kkc_ref.shape[3])
        v_tiles = _key_tiles(vo_ref, (0, 0), vo_ref.shape[2]) + _key_tiles(voc_ref, (0, 0, 0), voc_ref.shape[3])
    else:
        q_ref, kk_ref, vo_ref, qb2_ref, o_ref, r_ref, safe_ref, lhs_ref = refs
        k_tiles = _key_tiles(kk_ref, (0, 0), kk_ref.shape[2])
        v_tiles = _key_tiles(vo_ref, (0, 0), vo_ref.shape[2])
    tq = q_ref.shape[1]
    nq = QUERY_BLOCK
    lo = lax.broadcasted_iota(jnp.int32, (nq, LANES), 1) < HEAD_DIM
    zero = jnp.zeros((nq, LANES), BF16)

    def lhs_block(qb):
        q = q_ref[0, qb * nq:(qb + 1) * nq]
        qa, qc = q[:, :LANES], q[:, LANES:]
        return jnp.concatenate([jnp.where(lo, qa, zero), jnp.where(lo, zero, qa),
                                jnp.where(lo, qc, zero), jnp.where(lo, zero, qc)], axis=0)

    def finish(r, qb):
        rows = slice(qb * nq, (qb + 1) * nq)
        o_ref[0, rows, :LANES] = jnp.where(lo, r[0:nq], r[nq:2 * nq]).astype(BF16)
        o_ref[0, rows, LANES:] = jnp.where(lo, r[2 * nq:3 * nq], r[3 * nq:4 * nq]).astype(BF16)

    _attention(tq, lhs_block, finish, k_tiles, v_tiles, qb2_ref, layer, safe_ref, r_ref, lhs_ref)


def _gqa_call(qg, kk, vog, cache, layer, qb2, tq):
    b, t, _ = qg.shape
    nq = t // tq
    in_specs = [
        pl.BlockSpec((1, tq, 256), lambda i, h, j: (i, j, h)),
        pl.BlockSpec((1, 1, t, LANES), lambda i, h, j: (i, h, 0, 0)),
        pl.BlockSpec((1, 1, t, 2 * LANES), lambda i, h, j: (i, h, 0, 0)),
    ]
    args = [qg, kk, vog]
    if cache is not None:
        kkc, voc = cache
        s = kkc.shape[3]
        in_specs += [
            pl.BlockSpec((1, 1, 1, s, LANES), lambda i, h, j: (i, layer, h, 0, 0)),
            pl.BlockSpec((1, 1, 1, s, 2 * LANES), lambda i, h, j: (i, layer, h, 0, 0)),
        ]
        args += [kkc, voc]
    in_specs.append(_SMEM_SPEC)
    args.append(qb2)
    return pl.pallas_call(
        functools.partial(_gqa_kernel, has_cache=cache is not None, layer=layer),
        out_shape=jax.ShapeDtypeStruct((b, t, GQA_WIDTH), BF16),
        grid=(b, GQA_KV_HEADS, nq),
        in_specs=in_specs,
        out_specs=pl.BlockSpec((1, tq, 256), lambda i, h, j: (i, j, h)),
        scratch_shapes=_attn_scratch(tq),
        compiler_params=_attn_cparams(),
        name="gqa_attention",
    )(*args)


def _diff_kernel(*refs, has_cache, layer, lam_init):
    if has_cache:
        q_ref, k_ref, vo_ref, kc_ref, voc_ref, lam_ref, sg_ref, qb2_ref, o_ref, r_ref, safe_ref, lhs_ref = refs
        k_tiles = _key_tiles(k_ref, (0,), k_ref.shape[1]) + _key_tiles(kc_ref, (0, 0), kc_ref.shape[2])
        v_tiles = _key_tiles(vo_ref, (0, 0), vo_ref.shape[2]) + _key_tiles(voc_ref, (0, 0, 0), voc_ref.shape[3])
    else:
        q_ref, k_ref, vo_ref, lam_ref, sg_ref, qb2_ref, o_ref, r_ref, safe_ref, lhs_ref = refs
        k_tiles = _key_tiles(k_ref, (0,), k_ref.shape[1])
        v_tiles = _key_tiles(vo_ref, (0, 0), vo_ref.shape[2])
    tq = q_ref.shape[1]
    nq = QUERY_BLOCK
    lane = lax.broadcasted_iota(jnp.int32, (nq, LANES), 1)
    lo = lane < DIFF_V_DIM
    zero = jnp.zeros((nq, LANES), BF16)
    lf = lam_ref[...]
    lam = (jnp.exp(jnp.sum(lf[0:1] * lf[1:2], axis=-1, keepdims=True))
           - jnp.exp(jnp.sum(lf[2:3] * lf[3:4], axis=-1, keepdims=True)) + lam_init)
    gain = sg_ref[...] * (1.0 - lam_init)

    def lhs_block(qb):
        q = q_ref[0, qb * nq:(qb + 1) * nq]
        return jnp.concatenate(
            [jnp.where((lane >= DIFF_QK_DIM * u) & (lane < DIFF_QK_DIM * (u + 1)), q, zero) for u in range(4)],
            axis=0)

    def finish(r, qb):
        o = jnp.where(lo, r[0:nq] - lam * r[nq:2 * nq], r[2 * nq:3 * nq] - lam * r[3 * nq:4 * nq])
        sq = o * o
        ms_lo = jnp.sum(jnp.where(lo, sq, 0.0), axis=-1, keepdims=True) * (1.0 / DIFF_V_DIM)
        ms_hi = jnp.sum(jnp.where(lo, 0.0, sq), axis=-1, keepdims=True) * (1.0 / DIFF_V_DIM)
        inv = jnp.where(lo, lax.rsqrt(ms_lo + NORM_EPS), lax.rsqrt(ms_hi + NORM_EPS))
        o_ref[0, qb * nq:(qb + 1) * nq, :] = ((o * inv) * gain).astype(BF16)

    _attention(tq, lhs_block, finish, k_tiles, v_tiles, qb2_ref, layer, safe_ref, r_ref, lhs_ref)


def _diff_call(qd, kd, vod, cache, layer, lam_p, subg, qb2, lam_init, tq):
    b, t, _ = qd.shape
    nq = t // tq
    in_specs = [
        pl.BlockSpec((1, tq, LANES), lambda i, h, j: (i, j, h)),
        pl.BlockSpec((1, t, LANES), lambda i, h, j: (i, 0, h)),
        pl.BlockSpec((1, 1, t, 2 * LANES), lambda i, h, j: (i, h, 0, 0)),
    ]
    args = [qd, kd, vod]
    if cache is not None:
        kdc, vodc = cache
        s = kdc.shape[2]
        in_specs += [
            pl.BlockSpec((1, 1, s, LANES), lambda i, h, j: (i, layer, 0, h)),
            pl.BlockSpec((1, 1, 1, s, 2 * LANES), lambda i, h, j: (i, layer, h, 0, 0)),
        ]
        args += [kdc, vodc]
    in_specs += [_layer_spec((4, DIFF_QK_DIM), layer), _layer_spec((1, LANES), layer), _SMEM_SPEC]
    args += [lam_p, subg, qb2]
    return pl.pallas_call(
        functools.partial(_diff_kernel, has_cache=cache is not None, layer=layer, lam_init=lam_init),
        out_shape=jax.ShapeDtypeStruct((b, t, DIFF_WIDTH), BF16),
        grid=(b, 2, nq),
        in_specs=in_specs,
        out_specs=pl.BlockSpec((1, tq, LANES), lambda i, h, j: (i, j, h)),
        scratch_shapes=_attn_scratch(tq),
        compiler_params=_attn_cparams(),
        name="diff_attention",
    )(*args)


def _post_kernel(*refs, halo, tm, t_len):
    it = iter(refs)
    n_blk = 3 if halo else 1

    def take():
        blocks = [next(it) for _ in range(n_blk)]
        if halo:
            return jnp.concatenate([blocks[0][0], blocks[1][0], blocks[2][0]], axis=0)
        return blocks[0][0]

    xe, og, od, cb, ccu = take(), take(), take(), take(), take()
    mod_ref, cw_ref, cbias_ref, wout_ref, n2g_ref = (next(it) for _ in range(5))
    up_ref, fcw_ref, fcb_ref, down_ref = (next(it) for _ in range(4))
    out_ref, ccu_scr, a_scr = next(it), next(it), next(it)

    h0 = HALO if halo else 0
    rows = tm + 2 * h0
    mods = mod_ref[0]
    g1, sh2, sc2, g2 = mods[2:3], mods[3:4], mods[4:5], mods[5:6]

    if halo:
        pos = pl.program_id(1) * tm - h0 + lax.broadcasted_iota(jnp.int32, (rows, 1), 0)
        valid = (pos >= 0) & (pos < t_len)
        ccu = jnp.where(valid, ccu, 0.0)

    ccu_scr[0:PAD] = jnp.zeros((PAD, CONV_WIDTH), F32)
    ccu_scr[PAD + rows: 2 * PAD + rows] = jnp.zeros((PAD, CONV_WIDTH), F32)
    ccu_scr[PAD: PAD + rows] = ccu
    cw = cw_ref[...]
    conv = (ccu_scr[PAD - 1: PAD - 1 + rows] * cw[0:1] + ccu * cw[1:2]
            + ccu_scr[PAD + 1: PAD + 1 + rows] * cw[2:3] + cbias_ref[...])
    oc = (cb * conv).astype(BF16)

    y = (jnp.dot(og, wout_ref[0:GQA_WIDTH], preferred_element_type=F32)
         + jnp.dot(oc, wout_ref[GQA_WIDTH: GQA_WIDTH + CONV_WIDTH], preferred_element_type=F32)
         + jnp.dot(od, wout_ref[GQA_WIDTH + CONV_WIDTH:], preferred_element_type=F32))
    xmid = xe + g1 * y

    inv = lax.rsqrt(jnp.mean(xmid * xmid, axis=-1, keepdims=True) + NORM_EPS)
    h2 = ((xmid * inv) * n2g_ref[...] * (1.0 + sc2) + sh2).astype(BF16)
    a = jnp.dot(h2, up_ref[:, :D_FF], preferred_element_type=F32)
    if halo:
        a = jnp.where(valid, a, 0.0)
    a_scr[0:PAD] = jnp.zeros((PAD, D_FF), F32)
    a_scr[PAD + rows: 2 * PAD + rows] = jnp.zeros((PAD, D_FF), F32)
    a_scr[PAD: PAD + rows] = a
    base = PAD + h0
    fcw = fcw_ref[...]
    ac = (a_scr[base - 1: base - 1 + tm] * fcw[0:1] + a_scr[base: base + tm] * fcw[1:2]
          + a_scr[base + 1: base + 1 + tm] * fcw[2:3] + fcb_ref[...])
    u = jnp.dot(h2[h0: h0 + tm], up_ref[:, D_FF:], preferred_element_type=F32)
    f = (ac * jax.nn.sigmoid(ac) * u).astype(BF16)
    out_ref[0] = xmid[h0: h0 + tm] + g2 * jnp.dot(f, down_ref[...], preferred_element_type=F32)


def _post_call(x, og, od, cb, ccu, layer, mods, mod_row, cw, cbias, w_out, n2g, up, fcw, fcb, down, tm):
    b, t, d = x.shape
    nt = t // tm
    halo = nt > 1
    per = tm // HALO
    last = t // HALO - 1

    def specs(width):
        main = pl.BlockSpec((1, tm, width), lambda i, j: (i, j, 0))
        if not halo:
            return [main]
        prev = pl.BlockSpec((1, HALO, width), lambda i, j: (i, jnp.maximum(j * per - 1, 0), 0))
        nxt = pl.BlockSpec((1, HALO, width), lambda i, j: (i, jnp.minimum((j + 1) * per, last), 0))
        return [prev, main, nxt]

    in_specs, args = [], []
    for arr in (x, og, od, cb, ccu):
        sp = specs(arr.shape[-1])
        in_specs += sp
        args += [arr] * len(sp)
    in_specs += [
        _mods_spec(layer, mod_row),
        _layer_spec((3, CONV_WIDTH), layer), _layer_spec((1, CONV_WIDTH), layer),
        _layer_spec((d, d), layer), _layer_spec((1, d), layer),
        _layer_spec((d, 2 * D_FF), layer), _layer_spec((3, D_FF), layer), _layer_spec((1, D_FF), layer),
        _layer_spec((D_FF, d), layer),
    ]
    args += [mods, cw, cbias, w_out, n2g, up, fcw, fcb, down]
    rows = tm + (2 * HALO if halo else 0)
    return pl.pallas_call(
        functools.partial(_post_kernel, halo=halo, tm=tm, t_len=t),
        out_shape=jax.ShapeDtypeStruct((b, t, d), F32),
        grid=(b, nt),
        in_specs=in_specs,
        out_specs=pl.BlockSpec((1, tm, d), lambda i, j: (i, j, 0)),
        scratch_shapes=[pltpu.VMEM((rows + 2 * PAD, CONV_WIDTH), F32),
                        pltpu.VMEM((rows + 2 * PAD, D_FF), F32)],
        compiler_params=_cparams(),
        name="post_attention",
    )(*args)


def _rope_tables(n_rows, dim):
    row = jnp.repeat(jnp.arange(n_rows), GRID_W).astype(F32)
    col = jnp.tile(jnp.arange(GRID_W), n_rows).astype(F32)
    nf = dim // 4
    freqs = ROPE_THETA ** (-jnp.arange(nf, dtype=F32) / nf)
    ar = row[:, None] * freqs[None, :]
    ac = col[:, None] * freqs[None, :]
    ang = jnp.concatenate([ar, ar, ac, ac], axis=-1)
    cos, sin = jnp.cos(ang), jnp.sin(ang)
    first = ((jnp.arange(dim) // nf) % 2 == 0)[None, :]
    sin_up = jnp.where(first, -sin, 0.0)
    sin_dn = jnp.where(first, 0.0, sin)
    rep = LANES // dim
    return tuple(jnp.tile(tab, (1, rep)) for tab in (cos, sin_up, sin_dn))


def _group_matrix(group):
    idx = jnp.arange(256) // group
    return (idx[:, None] == idx[None, :]).astype(BF16)


def kernel(x_prompt, x_sample, cache_gqa_k, cache_gqa_v, cache_diff_k, cache_diff_v, c, c_ctx,
           w_mod, b_mod, norm1_g, w_in, gqa_qn_g, gqa_kn_g, conv_w, conv_b, diff_qn_g, diff_kn_g,
           diff_lambda, diff_subln_g, w_out, norm2_g, ffn_up, ffn_conv_w, ffn_conv_b, ffn_down):
    depth = w_in.shape[0]
    batch, seq, _ = x_prompt.shape
    dec_batch, dec_seq, _ = x_sample.shape

    cond_all = jnp.zeros((MOD_ROWS, D_MODEL), F32).at[0].set(c_ctx).at[1:1 + dec_batch].set(c)
    mods = _mods_call(cond_all, w_mod, b_mod)
    kkc, vogc, kdc, vodc = _cache_prep_call(cache_gqa_k, cache_gqa_v, cache_diff_k, cache_diff_v)

    w_in_b = w_in.astype(BF16)
    w_out_b = w_out.astype(BF16)
    up_b = ffn_up.astype(BF16)
    down_b = ffn_down.astype(BF16)
    g64, g32 = _group_matrix(HEAD_DIM), _group_matrix(DIFF_QK_DIM)
    rope_tabs = _rope_tables(dec_seq // GRID_W, HEAD_DIM) + _rope_tables(dec_seq // GRID_W, DIFF_QK_DIM)

    row = lambda p: p.reshape(depth, 1, -1)
    lane_tiled = lambda p, rep: jnp.tile(p, (1, rep)).reshape(depth, 1, LANES)
    gq = lane_tiled(gqa_qn_g * (HEAD_DIM ** -0.5 * LOG2E), 2)
    gk = lane_tiled(gqa_kn_g, 2)
    gqd = lane_tiled(diff_qn_g * (DIFF_QK_DIM ** -0.5 * LOG2E), 4)
    gkd = lane_tiled(diff_kn_g, 4)
    subg = lane_tiled(diff_subln_g, 2)
    qb2_g = _query_norm_bound(gq, HEAD_DIM)
    qb2_d = _query_norm_bound(gqd, DIFF_QK_DIM)
    n1g, n2g, cbias, fcb = row(norm1_g), row(norm2_g), row(conv_b), row(ffn_conv_b)

    def layer(x, l, mod_row, rope, cache_g, cache_d, own, tm, tq):
        lam_init = 0.8 - 0.6 * math.exp(-0.3 * l)
        pre = _pre_call(x, l, mods, mod_row, n1g, w_in_b, gq, gk, gqd, gkd, g64, g32, rope, own, tm)
        qg, kk, vog, qd, kd, vod, cb, ccu = pre[:8]
        og = _gqa_call(qg, kk, vog, cache_g, l, qb2_g, tq)
        od = _diff_call(qd, kd, vod, cache_d, l, diff_lambda, subg, qb2_d, lam_init, tq)
        x = _post_call(x, og, od, cb, ccu, l, mods, mod_row, conv_w, cbias, w_out_b, n2g, up_b,
                       ffn_conv_w, fcb, down_b, tm)
        return x, pre[8:]

    xp = x_prompt
    owns = []
    for l in range(depth):
        xp, own = layer(xp, l, lambda i: 0, None, None, None, True, seq, seq)
        owns.append(own)
    new_gqa_k = jnp.stack([o[0] for o in owns], axis=1).reshape(batch, depth, seq, GQA_KV_HEADS, HEAD_DIM)
    new_gqa_v = jnp.stack([o[1] for o in owns], axis=1).reshape(batch, depth, seq, GQA_KV_HEADS, HEAD_DIM)
    new_diff_k = jnp.stack([o[2] for o in owns], axis=1).reshape(batch, depth, seq, DIFF_HEADS, 2, DIFF_QK_DIM)
    new_diff_v = jnp.stack([o[3] for o in owns], axis=1).reshape(batch, depth, seq, DIFF_HEADS, DIFF_V_DIM)

    xs = x_sample
    for l in range(depth):
        xs, _ = layer(xs, l, lambda i: i + 1, rope_tabs, (kkc, vogc), (kdc, vodc), False, 512, 512)

    return (xp, xs, new_gqa_k, new_gqa_v, new_diff_k, new_diff_v)
```

```python
import functools
import math

import jax
import jax.numpy as jnp
from jax import lax
from jax.experimental import pallas as pl
from jax.experimental.pallas import tpu as pltpu

F32 = jnp.float32
BF16 = jnp.bfloat16

D_MODEL = 1024
GRID_W = 64
HEAD_DIM = 64
GQA_HEADS = 8
GQA_KV_HEADS = 2
GQA_WIDTH = GQA_HEADS * HEAD_DIM
CONV_WIDTH = 256
DIFF_HEADS = 4
DIFF_QK_DIM = 32
DIFF_V_DIM = 64
DIFF_WIDTH = DIFF_HEADS * DIFF_V_DIM
IN_WIDTH = 2304
D_FF = 2816
ROPE_THETA = 10000.0
NORM_EPS = 1e-6
N_MOD = 6
LANES = 128
MOD_ROWS = 16

OFF_QG, OFF_KG, OFF_VG = 0, 512, 640
OFF_CB, OFF_CC, OFF_CU = 768, 1024, 1280
OFF_QD, OFF_KD, OFF_VD = 1536, 1792, 2048

HALO = 16

VMEM_LIMIT = 60 * 1024 * 1024

KEY_TILE = 256
SAFE_EXPONENT = 60.0
LOG2E = math.log2(math.e)
PRE_ROWS = 128
QUERY_BLOCK = 64
EXACT_ROWS = 512


def _cparams():
    return pltpu.CompilerParams(vmem_limit_bytes=VMEM_LIMIT)


def _const_spec(shape):
    nd = len(shape)
    return pl.BlockSpec(shape, lambda *_: (0,) * nd, pipeline_mode=pl.Buffered(1))


def _layer_spec(shape, layer):
    nd = len(shape)
    return pl.BlockSpec((None,) + tuple(shape), lambda *_: (layer,) + (0,) * nd, pipeline_mode=pl.Buffered(1))


def _mods_spec(layer, mod_row):
    return pl.BlockSpec((None, 1, N_MOD, D_MODEL), lambda i, j: (layer, mod_row(i), 0, 0))


def _mods_kernel(cond_ref, w_ref, b_ref, o_ref):
    cnd = cond_ref[...]
    act = cnd * jax.nn.sigmoid(cnd)
    o_ref[0] = jnp.dot(act, w_ref[0], preferred_element_type=F32,
                       precision=lax.Precision.HIGHEST) + b_ref[0]


def _mods_call(cond_all, w_mod, b_mod):
    depth = w_mod.shape[0]
    width = N_MOD * D_MODEL
    tn = 1024
    out = pl.pallas_call(
        _mods_kernel,
        out_shape=jax.ShapeDtypeStruct((depth, MOD_ROWS, width), F32),
        grid=(depth, width // tn),
        in_specs=[
            pl.BlockSpec((MOD_ROWS, D_MODEL), lambda l, j: (0, 0)),
            pl.BlockSpec((1, D_MODEL, tn), lambda l, j: (l, 0, j)),
            pl.BlockSpec((1, 1, tn), lambda l, j: (l, 0, j)),
        ],
        out_specs=pl.BlockSpec((1, MOD_ROWS, tn), lambda l, j: (l, 0, j)),
        compiler_params=_cparams(),
        name="adaln_mods",
    )(cond_all, w_mod, b_mod.reshape(depth, 1, width))
    return out.reshape(depth, MOD_ROWS, N_MOD, D_MODEL)


def _dup_halves(x128):
    lo = lax.broadcasted_iota(jnp.int32, x128.shape, 1) < HEAD_DIM
    swapped = pltpu.roll(x128, HEAD_DIM, 1)
    return jnp.where(lo, x128, swapped), jnp.where(lo, swapped, x128)


def _cache_prep_kernel(gk_ref, gv_ref, dk_ref, dv_ref, kk_ref, vog_ref, kd_ref, vod_ref):
    ones = jnp.ones((gk_ref.shape[2], LANES), BF16)
    k0, k1 = _dup_halves(gk_ref[0, 0])
    kk_ref[0, 0, 0] = k0.astype(BF16)
    kk_ref[0, 0, 1] = k1.astype(BF16)
    v0, v1 = _dup_halves(gv_ref[0, 0])
    vog_ref[0, 0, 0, :, :LANES] = v0.astype(BF16)
    vog_ref[0, 0, 1, :, :LANES] = v1.astype(BF16)
    vog_ref[0, 0, 0, :, LANES:] = ones
    vog_ref[0, 0, 1, :, LANES:] = ones
    kd_ref[0, 0] = dk_ref[0, 0].astype(BF16)
    dv = dv_ref[0, 0]
    vod_ref[0, 0, 0, :, :LANES] = dv[:, :LANES].astype(BF16)
    vod_ref[0, 0, 1, :, :LANES] = dv[:, LANES:].astype(BF16)
    vod_ref[0, 0, 0, :, LANES:] = ones
    vod_ref[0, 0, 1, :, LANES:] = ones


def _cache_prep_call(cache_gqa_k, cache_gqa_v, cache_diff_k, cache_diff_v):
    b, depth, s = cache_gqa_k.shape[:3]
    gk = cache_gqa_k.reshape(b, depth, s, LANES)
    gv = cache_gqa_v.reshape(b, depth, s, LANES)
    dk = cache_diff_k.reshape(b, depth, s, 2 * LANES)
    dv = cache_diff_v.reshape(b, depth, s, 2 * LANES)
    idx4 = lambda i, l: (i, l, 0, 0)
    idx5 = lambda i, l: (i, l, 0, 0, 0)
    return pl.pallas_call(
        _cache_prep_kernel,
        out_shape=(
            jax.ShapeDtypeStruct((b, depth, 2, s, LANES), BF16),
            jax.ShapeDtypeStruct((b, depth, 2, s, 2 * LANES), BF16),
            jax.ShapeDtypeStruct((b, depth, s, 2 * LANES), BF16),
            jax.ShapeDtypeStruct((b, depth, 2, s, 2 * LANES), BF16),
        ),
        grid=(b, depth),
        in_specs=[
            pl.BlockSpec((1, 1, s, LANES), idx4),
            pl.BlockSpec((1, 1, s, LANES), idx4),
            pl.BlockSpec((1, 1, s, 2 * LANES), idx4),
            pl.BlockSpec((1, 1, s, 2 * LANES), idx4),
        ],
        out_specs=(
            pl.BlockSpec((1, 1, 2, s, LANES), idx5),
            pl.BlockSpec((1, 1, 2, s, 2 * LANES), idx5),
            pl.BlockSpec((1, 1, s, 2 * LANES), idx4),
            pl.BlockSpec((1, 1, 2, s, 2 * LANES), idx5),
        ),
        compiler_params=_cparams(),
        name="cache_prep",
    )(gk, gv, dk, dv)


def _group_inv_rms(z, g_ref, group):
    sq = z * z
    hi = sq.astype(BF16)
    lo = (sq - hi.astype(F32)).astype(BF16)
    width = z.shape[1]
    gmat = g_ref[:width, :width]
    ssum = (jnp.dot(hi, gmat, preferred_element_type=F32)
            + jnp.dot(lo, gmat, preferred_element_type=F32))
    return lax.rsqrt(ssum * (1.0 / group) + NORM_EPS)


def _rope_chunk(y, cos, sin_up, sin_dn, quarter):
    up = pltpu.roll(y, LANES - quarter, 1)
    dn = pltpu.roll(y, quarter, 1)
    return y * cos + up * sin_up + dn * sin_dn


def _pre_kernel(*refs, rope, own):
    it = iter(refs)
    x_ref, mod_ref, n1g_ref, w_in_ref = next(it), next(it), next(it), next(it)
    gq_ref, gk_ref, gqd_ref, gkd_ref = next(it), next(it), next(it), next(it)
    g64_ref, g32_ref = next(it), next(it)
    if rope:
        c64, su64, sd64, c32, su32, sd32 = (next(it) for _ in range(6))
    qg_ref, kk_ref, vog_ref, qd_ref, kd_ref, vod_ref, cb_ref, ccu_ref = (next(it) for _ in range(8))
    if own:
        kown_ref, vown_ref, kdown_ref, vdown_ref = (next(it) for _ in range(4))

    mods = mod_ref[0]
    sh1, sc1 = mods[0:1], mods[1:2]
    gain1 = n1g_ref[...] * (1.0 + sc1)
    ones = jnp.ones((PRE_ROWS, LANES), BF16)

    for r in range(x_ref.shape[1] // PRE_ROWS):
        rs = slice(r * PRE_ROWS, (r + 1) * PRE_ROWS)
        x = x_ref[0, rs]
        inv = lax.rsqrt(jnp.mean(x * x, axis=-1, keepdims=True) + NORM_EPS)
        h = (x * inv) * gain1 + sh1
        z = jnp.dot(h.astype(BF16), w_in_ref[...], preferred_element_type=F32)

        def rope64(y):
            if not rope:
                return y
            return _rope_chunk(y, c64[rs, :], su64[rs, :], sd64[rs, :], HEAD_DIM // 4)

        def rope32(y):
            if not rope:
                return y
            return _rope_chunk(y, c32[rs, :], su32[rs, :], sd32[rs, :], DIFF_QK_DIM // 4)

        for c in range(2):
            zc = z[:, OFF_QG + 256 * c: OFF_QG + 256 * (c + 1)]
            y = zc * _group_inv_rms(zc, g64_ref, HEAD_DIM)
            for j in range(2):
                yj = rope64(y[:, LANES * j: LANES * (j + 1)] * gq_ref[...])
                qg_ref[0, rs, 256 * c + LANES * j: 256 * c + LANES * (j + 1)] = yj.astype(BF16)

        zk = z[:, OFF_KG: OFF_KG + LANES]
        zv = z[:, OFF_VG: OFF_VG + LANES]
        kn = zk * _group_inv_rms(zk, g64_ref, HEAD_DIM) * gk_ref[...]
        if own:
            kown_ref[0, rs] = kn
            vown_ref[0, rs] = zv
        k0, k1 = _dup_halves(rope64(kn))
        kk_ref[0, 0, rs] = k0.astype(BF16)
        kk_ref[0, 1, rs] = k1.astype(BF16)
        v0, v1 = _dup_halves(zv)
        vog_ref[0, 0, rs, :LANES] = v0.astype(BF16)
        vog_ref[0, 1, rs, :LANES] = v1.astype(BF16)
        vog_ref[0, 0, rs, LANES:] = ones
        vog_ref[0, 1, rs, LANES:] = ones

        cb_ref[0, rs] = z[:, OFF_CB: OFF_CB + CONV_WIDTH]
        ccu_ref[0, rs] = z[:, OFF_CC: OFF_CC + CONV_WIDTH] * z[:, OFF_CU: OFF_CU + CONV_WIDTH]

        zq = z[:, OFF_QD: OFF_QD + 256]
        yq = zq * _group_inv_rms(zq, g32_ref, DIFF_QK_DIM)
        zkd = z[:, OFF_KD: OFF_KD + 256]
        ykd = zkd * _group_inv_rms(zkd, g32_ref, DIFF_QK_DIM)
        zvd = z[:, OFF_VD: OFF_VD + 256]
        for j in range(2):
            sl = slice(LANES * j, LANES * (j + 1))
            qd_ref[0, rs, sl] = rope32(yq[:, sl] * gqd_ref[...]).astype(BF16)
            kdj = ykd[:, sl] * gkd_ref[...]
            if own:
                kdown_ref[0, rs, sl] = kdj
            kd_ref[0, rs, sl] = rope32(kdj).astype(BF16)
            vod_ref[0, j, rs, :LANES] = zvd[:, sl].astype(BF16)
            vod_ref[0, j, rs, LANES:] = ones
        if own:
            vdown_ref[0, rs] = zvd


def _pre_call(x, layer, mods, mod_row, n1g, w_in, gq, gk, gqd, gkd, g64, g32, rope_tabs, own, tm):
    b, t, d = x.shape
    nt = t // tm
    rope = rope_tabs is not None
    gain_spec = _layer_spec((1, LANES), layer)
    in_specs = [
        pl.BlockSpec((1, tm, d), lambda i, j: (i, j, 0)),
        _mods_spec(layer, mod_row),
        _layer_spec((1, d), layer),
        _layer_spec((d, IN_WIDTH), layer),
        gain_spec, gain_spec, gain_spec, gain_spec,
        _const_spec((256, 256)), _const_spec((256, 256)),
    ]
    args = [x, mods, n1g, w_in, gq, gk, gqd, gkd, g64, g32]
    if rope:
        in_specs += [pl.BlockSpec((tm, LANES), lambda i, j: (j, 0))] * 6
        args += list(rope_tabs)
    tok = lambda w, dt: jax.ShapeDtypeStruct((b, t, w), dt)
    tok_spec = lambda w: pl.BlockSpec((1, tm, w), lambda i, j: (i, j, 0))
    pair = lambda w: jax.ShapeDtypeStruct((b, 2, t, w), BF16)
    pair_spec = lambda w: pl.BlockSpec((1, 2, tm, w), lambda i, j: (i, 0, j, 0))
    out_shape = [tok(GQA_WIDTH, BF16), pair(LANES), pair(2 * LANES), tok(256, BF16), tok(256, BF16),
                 pair(2 * LANES), tok(CONV_WIDTH, F32), tok(CONV_WIDTH, F32)]
    out_specs = [tok_spec(GQA_WIDTH), pair_spec(LANES), pair_spec(2 * LANES), tok_spec(256), tok_spec(256),
                 pair_spec(2 * LANES), tok_spec(CONV_WIDTH), tok_spec(CONV_WIDTH)]
    if own:
        out_shape += [tok(LANES, F32), tok(LANES, F32), tok(256, F32), tok(256, F32)]
        out_specs += [tok_spec(LANES), tok_spec(LANES), tok_spec(256), tok_spec(256)]
    return pl.pallas_call(
        functools.partial(_pre_kernel, rope=rope, own=own),
        out_shape=tuple(out_shape),
        grid=(b, nt),
        in_specs=in_specs,
        out_specs=tuple(out_specs),
        compiler_params=_cparams(),
        name="pre_attention",
    )(*args)


def _key_tiles(ref, lead, n_rows):
    def load(t):
        return ref[lead + (pl.ds(t * KEY_TILE, KEY_TILE), slice(None))]
    return [functools.partial(load, t) for t in range(n_rows // KEY_TILE)]


def _row_norm2_max(x):
    xf = x.astype(F32)
    return jnp.max(jnp.sum(xf * xf, axis=-1, keepdims=True))


def _dot_nt(a, b):
    return lax.dot_general(a, b, (((1,), (1,)), ((), ())), preferred_element_type=F32)


def _scores_are_safe(k_tiles, qb2_ref, layer, safe_ref, first_q_tile):
    @pl.when(first_q_tile)
    def _():
        kmax2 = functools.reduce(jnp.maximum, [_row_norm2_max(k()) for k in k_tiles])
        safe_ref[0] = (qb2_ref[layer] * kmax2 <= SAFE_EXPONENT ** 2).astype(jnp.int32)

    return safe_ref[0] == 1


def _softmax_pv_shifted(lhs_ref, k_tiles, v_tiles, r_ref):
    def row_block(rb, carry):
        rows = pl.ds(pl.multiple_of(rb * EXACT_ROWS, EXACT_ROWS), EXACT_ROWS)
        lhs_b = lhs_ref[rows, :]
        scores = [_dot_nt(lhs_b, k()) for k in k_tiles]
        m = functools.reduce(jnp.maximum, [jnp.max(s, axis=-1, keepdims=True) for s in scores])
        acc = None
        for s, v in zip(scores, v_tiles):
            p = jnp.exp2(s - m).astype(BF16)
            part = jnp.dot(p, v(), preferred_element_type=F32)
            acc = part if acc is None else acc + part
        r_ref[rows, :] = acc[:, :LANES] / acc[:, LANES:]
        return carry

    lax.fori_loop(0, lhs_ref.shape[0] // EXACT_ROWS, row_block, 0)


def _attention(tq, lhs_block, finish, k_tiles, v_tiles, qb2_ref, layer, safe_ref, r_ref, lhs_ref):
    n_blocks = tq // QUERY_BLOCK
    rows = 4 * QUERY_BLOCK
    safe = _scores_are_safe(k_tiles, qb2_ref, layer, safe_ref, pl.program_id(2) == 0)

    @pl.when(safe)
    def _():
        for qb in range(n_blocks):
            lhs_b = lhs_block(qb)
            acc = None
            for k, v in zip(k_tiles, v_tiles):
                p = jnp.exp2(_dot_nt(lhs_b, k())).astype(BF16)
                part = jnp.dot(p, v(), preferred_element_type=F32)
                acc = part if acc is None else acc + part
            finish(acc[:, :LANES] / acc[:, LANES:], qb)

    @pl.when(jnp.logical_not(safe))
    def _():
        for qb in range(n_blocks):
            lhs_ref[qb * rows:(qb + 1) * rows, :] = lhs_block(qb)
        _softmax_pv_shifted(lhs_ref, k_tiles, v_tiles, r_ref)
        for qb in range(n_blocks):
            finish(r_ref[qb * rows:(qb + 1) * rows, :], qb)


def _attn_scratch(tq):
    return [pltpu.VMEM((4 * tq, LANES), F32), pltpu.SMEM((1,), jnp.int32), pltpu.VMEM((4 * tq, LANES), BF16)]


def _query_norm_bound(gain, dim):
    return dim * 1.02 * jnp.max(gain * gain, axis=(1, 2))


_SMEM_SPEC = pl.BlockSpec(memory_space=pltpu.SMEM)


def _attn_cparams():
    return pltpu.CompilerParams(vmem_limit_bytes=VMEM_LIMIT,
                                dimension_semantics=("arbitrary", "arbitrary", "arbitrary"))


def _gqa_kernel(*refs, has_cache, layer):
    if has_cache:
        q_ref, kk_ref, vo_ref, kkc_ref, voc_ref, qb2_ref, o_ref, r_ref, safe_ref, lhs_ref = refs
        k_tiles = _key_tiles(kk_ref, (0, 0), kk_ref.shape[2]) + _key_tiles(kkc_ref, (0, 0, 0), kkc_ref.shape[3])
        v_tiles = _key_tiles(vo_ref, (0, 0), vo_ref.shape[2]) + _key_tiles(voc_ref, (0, 0, 0), voc_ref.shape[3])
    else:
        q_ref, kk_ref, vo_ref, qb2_ref, o_ref, r_ref, safe_ref, lhs_ref = refs
        k_tiles = _key_tiles(kk_ref, (0, 0), kk_ref.shape[2])
        v_tiles = _key_tiles(vo_ref, (0, 0), vo_ref.shape[2])
    tq = q_ref.shape[1]
    nq = QUERY_BLOCK
    lo = lax.broadcasted_iota(jnp.int32, (nq, LANES), 1) < HEAD_DIM
    zero = jnp.zeros((nq, LANES), BF16)

    def lhs_block(qb):
        q = q_ref[0, qb * nq:(qb + 1) * nq]
        qa, qc = q[:, :LANES], q[:, LANES:]
        return jnp.concatenate([jnp.where(lo, qa, zero), jnp.where(lo, zero, qa),
                                jnp.where(lo, qc, zero), jnp.where(lo, zero, qc)], axis=0)

    def finish(r, qb):
        rows = slice(qb * nq, (qb + 1) * nq)
        o_ref[0, rows, :LANES] = jnp.where(lo, r[0:nq], r[nq:2 * nq]).astype(BF16)
        o_ref[0, rows, LANES:] = jnp.where(lo, r[2 * nq:3 * nq], r[3 * nq:4 * nq]).astype(BF16)

    _attention(tq, lhs_block, finish, k_tiles, v_tiles, qb2_ref, layer, safe_ref, r_ref, lhs_ref)


def _gqa_call(qg, kk, vog, cache, layer, qb2, tq):
    b, t, _ = qg.shape
    nq = t // tq
    in_specs = [
        pl.BlockSpec((1, tq, 256), lambda i, h, j: (i, j, h)),
        pl.BlockSpec((1, 1, t, LANES), lambda i, h, j: (i, h, 0, 0)),
        pl.BlockSpec((1, 1, t, 2 * LANES), lambda i, h, j: (i, h, 0, 0)),
    ]
    args = [qg, kk, vog]
    if cache is not None:
        kkc, voc = cache
        s = kkc.shape[3]
        in_specs += [
            pl.BlockSpec((1, 1, 1, s, LANES), lambda i, h, j: (i, layer, h, 0, 0)),
            pl.BlockSpec((1, 1, 1, s, 2 * LANES), lambda i, h, j: (i, layer, h, 0, 0)),
        ]
        args += [kkc, voc]
    in_specs.append(_SMEM_SPEC)
    args.append(qb2)
    return pl.pallas_call(
        functools.partial(_gqa_kernel, has_cache=cache is not None, layer=layer),
        out_shape=jax.ShapeDtypeStruct((b, t, GQA_WIDTH), BF16),
        grid=(b, GQA_KV_HEADS, nq),
        in_specs=in_specs,
        out_specs=pl.BlockSpec((1, tq, 256), lambda i, h, j: (i, j, h)),
        scratch_shapes=_attn_scratch(tq),
        compiler_params=_attn_cparams(),
        name="gqa_attention",
    )(*args)


def _diff_kernel(*refs, has_cache, layer, lam_init):
    if has_cache:
        q_ref, k_ref, vo_ref, kc_ref, voc_ref, lam_ref, sg_ref, qb2_ref, o_ref, r_ref, safe_ref, lhs_ref = refs
        k_tiles = _key_tiles(k_ref, (0,), k_ref.shape[1]) + _key_tiles(kc_ref, (0, 0), kc_ref.shape[2])
        v_tiles = _key_tiles(vo_ref, (0, 0), vo_ref.shape[2]) + _key_tiles(voc_ref, (0, 0, 0), voc_ref.shape[3])
    else:
        q_ref, k_ref, vo_ref, lam_ref, sg_ref, qb2_ref, o_ref, r_ref, safe_ref, lhs_ref = refs
        k_tiles = _key_tiles(k_ref, (0,), k_ref.shape[1])
        v_tiles = _key_tiles(vo_ref, (0, 0), vo_ref.shape[2])
    tq = q_ref.shape[1]
    nq = QUERY_BLOCK
    lane = lax.broadcasted_iota(jnp.int32, (nq, LANES), 1)
    lo = lane < DIFF_V_DIM
    zero = jnp.zeros((nq, LANES), BF16)
    lf = lam_ref[...]
    lam = (jnp.exp(jnp.sum(lf[0:1] * lf[1:2], axis=-1, keepdims=True))
           - jnp.exp(jnp.sum(lf[2:3] * lf[3:4], axis=-1, keepdims=True)) + lam_init)
    gain = sg_ref[...] * (1.0 - lam_init)

    def lhs_block(qb):
        q = q_ref[0, qb * nq:(qb + 1) * nq]
        return jnp.concatenate(
            [jnp.where((lane >= DIFF_QK_DIM * u) & (lane < DIFF_QK_DIM * (u + 1)), q, zero) for u in range(4)],
            axis=0)

    def finish(r, qb):
        o = jnp.where(lo, r[0:nq] - lam * r[nq:2 * nq], r[2 * nq:3 * nq] - lam * r[3 * nq:4 * nq])
        sq = o * o
        ms_lo = jnp.sum(jnp.where(lo, sq, 0.0), axis=-1, keepdims=True) * (1.0 / DIFF_V_DIM)
        ms_hi = jnp.sum(jnp.where(lo, 0.0, sq), axis=-1, keepdims=True) * (1.0 / DIFF_V_DIM)
        inv = jnp.where(lo, lax.rsqrt(ms_lo + NORM_EPS), lax.rsqrt(ms_hi + NORM_EPS))
        o_ref[0, qb * nq:(qb + 1) * nq, :] = ((o * inv) * gain).astype(BF16)

    _attention(tq, lhs_block, finish, k_tiles, v_tiles, qb2_ref, layer, safe_ref, r_ref, lhs_ref)


def _diff_call(qd, kd, vod, cache, layer, lam_p, subg, qb2, lam_init, tq):
    b, t, _ = qd.shape
    nq = t // tq
    in_specs = [
        pl.BlockSpec((1, tq, LANES), lambda i, h, j: (i, j, h)),
        pl.BlockSpec((1, t, LANES), lambda i, h, j: (i, 0, h)),
        pl.BlockSpec((1, 1, t, 2 * LANES), lambda i, h, j: (i, h, 0, 0)),
    ]
    args = [qd, kd, vod]
    if cache is not None:
        kdc, vodc = cache
        s = kdc.shape[2]
        in_specs += [
            pl.BlockSpec((1, 1, s, LANES), lambda i, h, j: (i, layer, 0, h)),
            pl.BlockSpec((1, 1, 1, s, 2 * LANES), lambda i, h, j: (i, layer, h, 0, 0)),
        ]
        args += [kdc, vodc]
    in_specs += [_layer_spec((4, DIFF_QK_DIM), layer), _layer_spec((1, LANES), layer), _SMEM_SPEC]
    args += [lam_p, subg, qb2]
    return pl.pallas_call(
        functools.partial(_diff_kernel, has_cache=cache is not None, layer=layer, lam_init=lam_init),
        out_shape=jax.ShapeDtypeStruct((b, t, DIFF_WIDTH), BF16),
        grid=(b, 2, nq),
        in_specs=in_specs,
        out_specs=pl.BlockSpec((1, tq, LANES), lambda i, h, j: (i, j, h)),
        scratch_shapes=_attn_scratch(tq),
        compiler_params=_attn_cparams(),
        name="diff_attention",
    )(*args)


def _post_kernel(*refs, halo, tm, t_len):
    it = iter(refs)
    n_blk = 3 if halo else 1

    def take():
        blocks = [next(it) for _ in range(n_blk)]
        if halo:
            return jnp.concatenate([blocks[0][0], blocks[1][0], blocks[2][0]], axis=0)
        return blocks[0][0]

    xe, og, od, cb, ccu = take(), take(), take(), take(), take()
    mod_ref, cw_ref, cbias_ref, wout_ref, n2g_ref = (next(it) for _ in range(5))
    up_ref, fcw_ref, fcb_ref, down_ref = (next(it) for _ in range(4))
    out_ref = next(it)

    h0 = HALO if halo else 0
    rows = tm + 2 * h0
    mods = mod_ref[0]
    g1, sh2, sc2, g2 = mods[2:3], mods[3:4], mods[4:5], mods[5:6]

    if halo:
        pos = pl.program_id(1) * tm - h0 + lax.broadcasted_iota(jnp.int32, (rows, 1), 0)
        valid = (pos >= 0) & (pos < t_len)
        ccu = jnp.where(valid, ccu, 0.0)

    def neighbours(v):
        prev, nxt = pltpu.roll(v, 1, 0), pltpu.roll(v, rows - 1, 0)
        if not halo:
            row = lax.broadcasted_iota(jnp.int32, (rows, 1), 0)
            prev, nxt = jnp.where(row == 0, 0.0, prev), jnp.where(row == rows - 1, 0.0, nxt)
        return prev, nxt

    cw = cw_ref[...]
    ccu_prev, ccu_next = neighbours(ccu)
    conv = ccu_prev * cw[0:1] + ccu * cw[1:2] + ccu_next * cw[2:3] + cbias_ref[...]
    oc = (cb * conv).astype(BF16)

    y = (jnp.dot(og, wout_ref[0:GQA_WIDTH], preferred_element_type=F32)
         + jnp.dot(oc, wout_ref[GQA_WIDTH: GQA_WIDTH + CONV_WIDTH], preferred_element_type=F32)
         + jnp.dot(od, wout_ref[GQA_WIDTH + CONV_WIDTH:], preferred_element_type=F32))
    xmid = xe + g1 * y

    inv = lax.rsqrt(jnp.mean(xmid * xmid, axis=-1, keepdims=True) + NORM_EPS)
    h2 = ((xmid * inv) * n2g_ref[...] * (1.0 + sc2) + sh2).astype(BF16)
    a = jnp.dot(h2, up_ref[:, :D_FF], preferred_element_type=F32)
    if halo:
        a = jnp.where(valid, a, 0.0)
    main = slice(h0, h0 + tm)
    a_prev, a_next = neighbours(a)
    fcw = fcw_ref[...]
    ac = a_prev[main] * fcw[0:1] + a[main] * fcw[1:2] + a_next[main] * fcw[2:3] + fcb_ref[...]
    u = jnp.dot(h2[main], up_ref[:, D_FF:], preferred_element_type=F32)
    f = (ac * jax.nn.sigmoid(ac) * u).astype(BF16)
    out_ref[0] = xmid[main] + g2 * jnp.dot(f, down_ref[...], preferred_element_type=F32)


def _post_call(x, og, od, cb, ccu, layer, mods, mod_row, cw, cbias, w_out, n2g, up, fcw, fcb, down, tm):
    b, t, d = x.shape
    nt = t // tm
    halo = nt > 1
    per = tm // HALO
    last = t // HALO - 1

    def specs(width):
        main = pl.BlockSpec((1, tm, width), lambda i, j: (i, j, 0))
        if not halo:
            return [main]
        prev = pl.BlockSpec((1, HALO, width), lambda i, j: (i, jnp.maximum(j * per - 1, 0), 0))
        nxt = pl.BlockSpec((1, HALO, width), lambda i, j: (i, jnp.minimum((j + 1) * per, last), 0))
        return [prev, main, nxt]

    in_specs, args = [], []
    for arr in (x, og, od, cb, ccu):
        sp = specs(arr.shape[-1])
        in_specs += sp
        args += [arr] * len(sp)
    in_specs += [
        _mods_spec(layer, mod_row),
        _layer_spec((3, CONV_WIDTH), layer), _layer_spec((1, CONV_WIDTH), layer),
        _layer_spec((d, d), layer), _layer_spec((1, d), layer),
        _layer_spec((d, 2 * D_FF), layer), _layer_spec((3, D_FF), layer), _layer_spec((1, D_FF), layer),
        _layer_spec((D_FF, d), layer),
    ]
    args += [mods, cw, cbias, w_out, n2g, up, fcw, fcb, down]
    return pl.pallas_call(
        functools.partial(_post_kernel, halo=halo, tm=tm, t_len=t),
        out_shape=jax.ShapeDtypeStruct((b, t, d), F32),
        grid=(b, nt),
        in_specs=in_specs,
        out_specs=pl.BlockSpec((1, tm, d), lambda i, j: (i, j, 0)),
        compiler_params=_cparams(),
        name="post_attention",
    )(*args)


def _rope_tables(n_rows, dim):
    row = jnp.repeat(jnp.arange(n_rows), GRID_W).astype(F32)
    col = jnp.tile(jnp.arange(GRID_W), n_rows).astype(F32)
    nf = dim // 4
    freqs = ROPE_THETA ** (-jnp.arange(nf, dtype=F32) / nf)
    ar = row[:, None] * freqs[None, :]
    ac = col[:, None] * freqs[None, :]
    ang = jnp.concatenate([ar, ar, ac, ac], axis=-1)
    cos, sin = jnp.cos(ang), jnp.sin(ang)
    first = ((jnp.arange(dim) // nf) % 2 == 0)[None, :]
    sin_up = jnp.where(first, -sin, 0.0)
    sin_dn = jnp.where(first, 0.0, sin)
    rep = LANES // dim
    return tuple(jnp.tile(tab, (1, rep)) for tab in (cos, sin_up, sin_dn))


def _group_matrix(group):
    idx = jnp.arange(256) // group
    return (idx[:, None] == idx[None, :]).astype(BF16)


def kernel(x_prompt, x_sample, cache_gqa_k, cache_gqa_v, cache_diff_k, cache_diff_v, c, c_ctx,
           w_mod, b_mod, norm1_g, w_in, gqa_qn_g, gqa_kn_g, conv_w, conv_b, diff_qn_g, diff_kn_g,
           diff_lambda, diff_subln_g, w_out, norm2_g, ffn_up, ffn_conv_w, ffn_conv_b, ffn_down):
    depth = w_in.shape[0]
    batch, seq, _ = x_prompt.shape
    dec_batch, dec_seq, _ = x_sample.shape

    cond_all = jnp.zeros((MOD_ROWS, D_MODEL), F32).at[0].set(c_ctx).at[1:1 + dec_batch].set(c)
    mods = _mods_call(cond_all, w_mod, b_mod)
    kkc, vogc, kdc, vodc = _cache_prep_call(cache_gqa_k, cache_gqa_v, cache_diff_k, cache_diff_v)

    w_in_b = w_in.astype(BF16)
    w_out_b = w_out.astype(BF16)
    up_b = ffn_up.astype(BF16)
    down_b = ffn_down.astype(BF16)
    g64, g32 = _group_matrix(HEAD_DIM), _group_matrix(DIFF_QK_DIM)
    rope_tabs = _rope_tables(dec_seq // GRID_W, HEAD_DIM) + _rope_tables(dec_seq // GRID_W, DIFF_QK_DIM)

    row = lambda p: p.reshape(depth, 1, -1)
    lane_tiled = lambda p, rep: jnp.tile(p, (1, rep)).reshape(depth, 1, LANES)
    gq = lane_tiled(gqa_qn_g * (HEAD_DIM ** -0.5 * LOG2E), 2)
    gk = lane_tiled(gqa_kn_g, 2)
    gqd = lane_tiled(diff_qn_g * (DIFF_QK_DIM ** -0.5 * LOG2E), 4)
    gkd = lane_tiled(diff_kn_g, 4)
    subg = lane_tiled(diff_subln_g, 2)
    qb2_g = _query_norm_bound(gq, HEAD_DIM)
    qb2_d = _query_norm_bound(gqd, DIFF_QK_DIM)
    n1g, n2g, cbias, fcb = row(norm1_g), row(norm2_g), row(conv_b), row(ffn_conv_b)

    def layer(x, l, mod_row, rope, cache_g, cache_d, own, tp, tm, tq):
        lam_init = 0.8 - 0.6 * math.exp(-0.3 * l)
        pre = _pre_call(x, l, mods, mod_row, n1g, w_in_b, gq, gk, gqd, gkd, g64, g32, rope, own, tp)
        qg, kk, vog, qd, kd, vod, cb, ccu = pre[:8]
        og = _gqa_call(qg, kk, vog, cache_g, l, qb2_g, tq)
        od = _diff_call(qd, kd, vod, cache_d, l, diff_lambda, subg, qb2_d, lam_init, tq)
        x = _post_call(x, og, od, cb, ccu, l, mods, mod_row, conv_w, cbias, w_out_b, n2g, up_b,
                       ffn_conv_w, fcb, down_b, tm)
        return x, pre[8:]

    xp = x_prompt
    owns = []
    for l in range(depth):
        xp, own = layer(xp, l, lambda i: 0, None, None, None, True, seq, seq, seq)
        owns.append(own)
    new_gqa_k = jnp.stack([o[0] for o in owns], axis=1).reshape(batch, depth, seq, GQA_KV_HEADS, HEAD_DIM)
    new_gqa_v = jnp.stack([o[1] for o in owns], axis=1).reshape(batch, depth, seq, GQA_KV_HEADS, HEAD_DIM)
    new_diff_k = jnp.stack([o[2] for o in owns], axis=1).reshape(batch, depth, seq, DIFF_HEADS, 2, DIFF_QK_DIM)
    new_diff_v = jnp.stack([o[3] for o in owns], axis=1).reshape(batch, depth, seq, DIFF_HEADS, DIFF_V_DIM)

    xs = x_sample
    for l in range(depth):
        xs, _ = layer(xs, l, lambda i: i + 1, rope_tabs, (kkc, vogc), (kdc, vodc), False, 1024, 512, 512)

    return (xp, xs, new_gqa_k, new_gqa_v, new_diff_k, new_diff_v)
```

```python
import functools
import math

import jax
import jax.numpy as jnp
from jax import lax
from jax.experimental import pallas as pl
from jax.experimental.pallas import tpu as pltpu

F32 = jnp.float32
BF16 = jnp.bfloat16

D_MODEL = 1024
GRID_W = 64
HEAD_DIM = 64
GQA_HEADS = 8
GQA_KV_HEADS = 2
GQA_WIDTH = GQA_HEADS * HEAD_DIM
CONV_WIDTH = 256
DIFF_HEADS = 4
DIFF_QK_DIM = 32
DIFF_V_DIM = 64
DIFF_WIDTH = DIFF_HEADS * DIFF_V_DIM
IN_WIDTH = 2304
D_FF = 2816
ROPE_THETA = 10000.0
NORM_EPS = 1e-6
N_MOD = 6
LANES = 128
MOD_ROWS = 16

OFF_QG, OFF_KG, OFF_VG = 0, 512, 640
OFF_CB, OFF_CC, OFF_CU = 768, 1024, 1280
OFF_QD, OFF_KD, OFF_VD = 1536, 1792, 2048

HALO = 16
PAD = 8

VMEM_LIMIT = 60 * 1024 * 1024

KEY_TILE = 256
SAFE_EXPONENT = 60.0
LOG2E = math.log2(math.e)
PRE_ROWS = 128
QUERY_BLOCK = 64
EXACT_ROWS = 512


def _cparams():
    return pltpu.CompilerParams(vmem_limit_bytes=VMEM_LIMIT)


def _const_spec(shape):
    nd = len(shape)
    return pl.BlockSpec(shape, lambda *_: (0,) * nd, pipeline_mode=pl.Buffered(1))


def _layer_spec(shape, layer):
    nd = len(shape)
    return pl.BlockSpec((None,) + tuple(shape), lambda *_: (layer,) + (0,) * nd, pipeline_mode=pl.Buffered(1))


def _mods_spec(layer, mod_row):
    return pl.BlockSpec((None, 1, N_MOD, D_MODEL), lambda i, j: (layer, mod_row(i), 0, 0))


def _mods_kernel(cond_ref, w_ref, b_ref, o_ref):
    cnd = cond_ref[...]
    act = cnd * jax.nn.sigmoid(cnd)
    o_ref[0] = jnp.dot(act, w_ref[0], preferred_element_type=F32,
                       precision=lax.Precision.HIGHEST) + b_ref[0]


def _mods_call(cond_all, w_mod, b_mod):
    depth = w_mod.shape[0]
    width = N_MOD * D_MODEL
    tn = 1024
    out = pl.pallas_call(
        _mods_kernel,
        out_shape=jax.ShapeDtypeStruct((depth, MOD_ROWS, width), F32),
        grid=(depth, width // tn),
        in_specs=[
            pl.BlockSpec((MOD_ROWS, D_MODEL), lambda l, j: (0, 0)),
            pl.BlockSpec((1, D_MODEL, tn), lambda l, j: (l, 0, j)),
            pl.BlockSpec((1, 1, tn), lambda l, j: (l, 0, j)),
        ],
        out_specs=pl.BlockSpec((1, MOD_ROWS, tn), lambda l, j: (l, 0, j)),
        compiler_params=_cparams(),
        name="adaln_mods",
    )(cond_all, w_mod, b_mod.reshape(depth, 1, width))
    return out.reshape(depth, MOD_ROWS, N_MOD, D_MODEL)


def _dup_halves(x128):
    lo = lax.broadcasted_iota(jnp.int32, x128.shape, 1) < HEAD_DIM
    swapped = pltpu.roll(x128, HEAD_DIM, 1)
    return jnp.where(lo, x128, swapped), jnp.where(lo, swapped, x128)


def _cache_prep_kernel(gk_ref, gv_ref, dk_ref, dv_ref, kk_ref, vog_ref, kd_ref, vod_ref):
    ones = jnp.ones((gk_ref.shape[2], LANES), BF16)
    k0, k1 = _dup_halves(gk_ref[0, 0])
    kk_ref[0, 0, 0] = k0.astype(BF16)
    kk_ref[0, 0, 1] = k1.astype(BF16)
    v0, v1 = _dup_halves(gv_ref[0, 0])
    vog_ref[0, 0, 0, :, :LANES] = v0.astype(BF16)
    vog_ref[0, 0, 1, :, :LANES] = v1.astype(BF16)
    vog_ref[0, 0, 0, :, LANES:] = ones
    vog_ref[0, 0, 1, :, LANES:] = ones
    kd_ref[0, 0] = dk_ref[0, 0].astype(BF16)
    dv = dv_ref[0, 0]
    vod_ref[0, 0, 0, :, :LANES] = dv[:, :LANES].astype(BF16)
    vod_ref[0, 0, 1, :, :LANES] = dv[:, LANES:].astype(BF16)
    vod_ref[0, 0, 0, :, LANES:] = ones
    vod_ref[0, 0, 1, :, LANES:] = ones


def _cache_prep_call(cache_gqa_k, cache_gqa_v, cache_diff_k, cache_diff_v):
    b, depth, s = cache_gqa_k.shape[:3]
    gk = cache_gqa_k.reshape(b, depth, s, LANES)
    gv = cache_gqa_v.reshape(b, depth, s, LANES)
    dk = cache_diff_k.reshape(b, depth, s, 2 * LANES)
    dv = cache_diff_v.reshape(b, depth, s, 2 * LANES)
    idx4 = lambda i, l: (i, l, 0, 0)
    idx5 = lambda i, l: (i, l, 0, 0, 0)
    return pl.pallas_call(
        _cache_prep_kernel,
        out_shape=(
            jax.ShapeDtypeStruct((b, depth, 2, s, LANES), BF16),
            jax.ShapeDtypeStruct((b, depth, 2, s, 2 * LANES), BF16),
            jax.ShapeDtypeStruct((b, depth, s, 2 * LANES), BF16),
            jax.ShapeDtypeStruct((b, depth, 2, s, 2 * LANES), BF16),
        ),
        grid=(b, depth),
        in_specs=[
            pl.BlockSpec((1, 1, s, LANES), idx4),
            pl.BlockSpec((1, 1, s, LANES), idx4),
            pl.BlockSpec((1, 1, s, 2 * LANES), idx4),
            pl.BlockSpec((1, 1, s, 2 * LANES), idx4),
        ],
        out_specs=(
            pl.BlockSpec((1, 1, 2, s, LANES), idx5),
            pl.BlockSpec((1, 1, 2, s, 2 * LANES), idx5),
            pl.BlockSpec((1, 1, s, 2 * LANES), idx4),
            pl.BlockSpec((1, 1, 2, s, 2 * LANES), idx5),
        ),
        compiler_params=_cparams(),
        name="cache_prep",
    )(gk, gv, dk, dv)


def _group_inv_rms(z, g_ref, group):
    sq = z * z
    hi = sq.astype(BF16)
    lo = (sq - hi.astype(F32)).astype(BF16)
    width = z.shape[1]
    gmat = g_ref[:width, :width]
    ssum = (jnp.dot(hi, gmat, preferred_element_type=F32)
            + jnp.dot(lo, gmat, preferred_element_type=F32))
    return lax.rsqrt(ssum * (1.0 / group) + NORM_EPS)


def _rope_chunk(y, cos, sin_up, sin_dn, quarter):
    up = pltpu.roll(y, LANES - quarter, 1)
    dn = pltpu.roll(y, quarter, 1)
    return y * cos + up * sin_up + dn * sin_dn


def _pre_kernel(*refs, rope, own):
    it = iter(refs)
    x_ref, mod_ref, n1g_ref, w_in_ref = next(it), next(it), next(it), next(it)
    gq_ref, gk_ref, gqd_ref, gkd_ref = next(it), next(it), next(it), next(it)
    g64_ref, g32_ref = next(it), next(it)
    if rope:
        c64, su64, sd64, c32, su32, sd32 = (next(it) for _ in range(6))
    qg_ref, kk_ref, vog_ref, qd_ref, kd_ref, vod_ref, cb_ref, ccu_ref = (next(it) for _ in range(8))
    if own:
        kown_ref, vown_ref, kdown_ref, vdown_ref = (next(it) for _ in range(4))

    mods = mod_ref[0]
    sh1, sc1 = mods[0:1], mods[1:2]
    gain1 = n1g_ref[...] * (1.0 + sc1)
    ones = jnp.ones((PRE_ROWS, LANES), BF16)

    for r in range(x_ref.shape[1] // PRE_ROWS):
        rs = slice(r * PRE_ROWS, (r + 1) * PRE_ROWS)
        x = x_ref[0, rs]
        inv = lax.rsqrt(jnp.mean(x * x, axis=-1, keepdims=True) + NORM_EPS)
        h = (x * inv) * gain1 + sh1
        z = jnp.dot(h.astype(BF16), w_in_ref[...], preferred_element_type=F32)

        def rope64(y):
            if not rope:
                return y
            return _rope_chunk(y, c64[rs, :], su64[rs, :], sd64[rs, :], HEAD_DIM // 4)

        def rope32(y):
            if not rope:
                return y
            return _rope_chunk(y, c32[rs, :], su32[rs, :], sd32[rs, :], DIFF_QK_DIM // 4)

        for c in range(2):
            zc = z[:, OFF_QG + 256 * c: OFF_QG + 256 * (c + 1)]
            y = zc * _group_inv_rms(zc, g64_ref, HEAD_DIM)
            for j in range(2):
                yj = rope64(y[:, LANES * j: LANES * (j + 1)] * gq_ref[...])
                qg_ref[0, rs, 256 * c + LANES * j: 256 * c + LANES * (j + 1)] = yj.astype(BF16)

        zk = z[:, OFF_KG: OFF_KG + LANES]
        zv = z[:, OFF_VG: OFF_VG + LANES]
        kn = zk * _group_inv_rms(zk, g64_ref, HEAD_DIM) * gk_ref[...]
        if own:
            kown_ref[0, rs] = kn
            vown_ref[0, rs] = zv
        k0, k1 = _dup_halves(rope64(kn))
        kk_ref[0, 0, rs] = k0.astype(BF16)
        kk_ref[0, 1, rs] = k1.astype(BF16)
        v0, v1 = _dup_halves(zv)
        vog_ref[0, 0, rs, :LANES] = v0.astype(BF16)
        vog_ref[0, 1, rs, :LANES] = v1.astype(BF16)
        vog_ref[0, 0, rs, LANES:] = ones
        vog_ref[0, 1, rs, LANES:] = ones

        cb_ref[0, rs] = z[:, OFF_CB: OFF_CB + CONV_WIDTH]
        ccu_ref[0, rs] = z[:, OFF_CC: OFF_CC + CONV_WIDTH] * z[:, OFF_CU: OFF_CU + CONV_WIDTH]

        zq = z[:, OFF_QD: OFF_QD + 256]
        yq = zq * _group_inv_rms(zq, g32_ref, DIFF_QK_DIM)
        zkd = z[:, OFF_KD: OFF_KD + 256]
        ykd = zkd * _group_inv_rms(zkd, g32_ref, DIFF_QK_DIM)
        zvd = z[:, OFF_VD: OFF_VD + 256]
        for j in range(2):
            sl = slice(LANES * j, LANES * (j + 1))
            qd_ref[0, rs, sl] = rope32(yq[:, sl] * gqd_ref[...]).astype(BF16)
            kdj = ykd[:, sl] * gkd_ref[...]
            if own:
                kdown_ref[0, rs, sl] = kdj
            kd_ref[0, rs, sl] = rope32(kdj).astype(BF16)
            vod_ref[0, j, rs, :LANES] = zvd[:, sl].astype(BF16)
            vod_ref[0, j, rs, LANES:] = ones
        if own:
            vdown_ref[0, rs] = zvd


def _pre_call(x, layer, mods, mod_row, n1g, w_in, gq, gk, gqd, gkd, g64, g32, rope_tabs, own, tm):
    b, t, d = x.shape
    nt = t // tm
    rope = rope_tabs is not None
    gain_spec = _layer_spec((1, LANES), layer)
    in_specs = [
        pl.BlockSpec((1, tm, d), lambda i, j: (i, j, 0)),
        _mods_spec(layer, mod_row),
        _layer_spec((1, d), layer),
        _layer_spec((d, IN_WIDTH), layer),
        gain_spec, gain_spec, gain_spec, gain_spec,
        _const_spec((256, 256)), _const_spec((256, 256)),
    ]
    args = [x, mods, n1g, w_in, gq, gk, gqd, gkd, g64, g32]
    if rope:
        in_specs += [pl.BlockSpec((tm, LANES), lambda i, j: (j, 0))] * 6
        args += list(rope_tabs)
    tok = lambda w, dt: jax.ShapeDtypeStruct((b, t, w), dt)
    tok_spec = lambda w: pl.BlockSpec((1, tm, w), lambda i, j: (i, j, 0))
    pair = lambda w: jax.ShapeDtypeStruct((b, 2, t, w), BF16)
    pair_spec = lambda w: pl.BlockSpec((1, 2, tm, w), lambda i, j: (i, 0, j, 0))
    out_shape = [tok(GQA_WIDTH, BF16), pair(LANES), pair(2 * LANES), tok(256, BF16), tok(256, BF16),
                 pair(2 * LANES), tok(CONV_WIDTH, F32), tok(CONV_WIDTH, F32)]
    out_specs = [tok_spec(GQA_WIDTH), pair_spec(LANES), pair_spec(2 * LANES), tok_spec(256), tok_spec(256),
                 pair_spec(2 * LANES), tok_spec(CONV_WIDTH), tok_spec(CONV_WIDTH)]
    if own:
        out_shape += [tok(LANES, F32), tok(LANES, F32), tok(256, F32), tok(256, F32)]
        out_specs += [tok_spec(LANES), tok_spec(LANES), tok_spec(256), tok_spec(256)]
    return pl.pallas_call(
        functools.partial(_pre_kernel, rope=rope, own=own),
        out_shape=tuple(out_shape),
        grid=(b, nt),
        in_specs=in_specs,
        out_specs=tuple(out_specs),
        compiler_params=_cparams(),
        name="pre_attention",
    )(*args)


def _key_tiles(ref, lead, n_rows, lanes=slice(None)):
    def load(t):
        return ref[lead + (pl.ds(t * KEY_TILE, KEY_TILE), lanes)]
    return [functools.partial(load, t) for t in range(n_rows // KEY_TILE)]


def _row_norm2_max(x):
    xf = x.astype(F32)
    return jnp.max(jnp.sum(xf * xf, axis=-1, keepdims=True))


def _dot_nt(a, b):
    return lax.dot_general(a, b, (((1,), (1,)), ((), ())), preferred_element_type=F32)


def _scores_are_safe(k_tiles, qb2):
    kmax2 = functools.reduce(jnp.maximum, [_row_norm2_max(k()) for k in k_tiles])
    return qb2 * kmax2 <= SAFE_EXPONENT ** 2


def _softmax_pv_shifted(lhs_ref, k_tiles, v_tiles, r_ref):
    def row_block(rb, carry):
        rows = pl.ds(pl.multiple_of(rb * EXACT_ROWS, EXACT_ROWS), EXACT_ROWS)
        lhs_b = lhs_ref[rows, :]
        scores = [_dot_nt(lhs_b, k()) for k in k_tiles]
        m = functools.reduce(jnp.maximum, [jnp.max(s, axis=-1, keepdims=True) for s in scores])
        acc = None
        for s, v in zip(scores, v_tiles):
            p = jnp.exp2(s - m).astype(BF16)
            part = jnp.dot(p, v(), preferred_element_type=F32)
            acc = part if acc is None else acc + part
        r_ref[rows, :] = acc[:, :LANES] / acc[:, LANES:]
        return carry

    lax.fori_loop(0, lhs_ref.shape[0] // EXACT_ROWS, row_block, 0)


def _group_unshifted(tq, group):
    lhs_block, finish, k_tiles, v_tiles = group
    for qb in range(tq // QUERY_BLOCK):
        lhs_b = lhs_block(qb)
        acc = None
        for k, v in zip(k_tiles, v_tiles):
            p = jnp.exp2(_dot_nt(lhs_b, k())).astype(BF16)
            part = jnp.dot(p, v(), preferred_element_type=F32)
            acc = part if acc is None else acc + part
        finish(acc[:, :LANES] / acc[:, LANES:], qb)


def _group_shifted(tq, group, r_ref, lhs_ref):
    lhs_block, finish, k_tiles, v_tiles = group
    rows = 4 * QUERY_BLOCK
    for qb in range(tq // QUERY_BLOCK):
        lhs_ref[qb * rows:(qb + 1) * rows, :] = lhs_block(qb)
    _softmax_pv_shifted(lhs_ref, k_tiles, v_tiles, r_ref)
    for qb in range(tq // QUERY_BLOCK):
        finish(r_ref[qb * rows:(qb + 1) * rows, :], qb)


def _attention(tq, groups, safes, r_ref, lhs_ref):
    all_safe = functools.reduce(jnp.logical_and, safes)

    @pl.when(all_safe)
    def _():
        for group in groups:
            _group_unshifted(tq, group)

    @pl.when(jnp.logical_not(all_safe))
    def _():
        for group, safe in zip(groups, safes):
            if len(groups) > 1:
                pl.when(safe)(functools.partial(_group_unshifted, tq, group))
            pl.when(jnp.logical_not(safe))(functools.partial(_group_shifted, tq, group, r_ref, lhs_ref))


def _gqa_group(q_ref, o_ref, lane0, k_tiles, v_tiles):
    nq = QUERY_BLOCK
    lo = lax.broadcasted_iota(jnp.int32, (nq, LANES), 1) < HEAD_DIM
    zero = jnp.zeros((nq, LANES), BF16)

    def lhs_block(qb):
        rows = slice(qb * nq, (qb + 1) * nq)
        qa = q_ref[0, rows, lane0:lane0 + LANES]
        qc = q_ref[0, rows, lane0 + LANES:lane0 + 2 * LANES]
        return jnp.concatenate([jnp.where(lo, qa, zero), jnp.where(lo, zero, qa),
                                jnp.where(lo, qc, zero), jnp.where(lo, zero, qc)], axis=0)

    def finish(r, qb):
        rows = slice(qb * nq, (qb + 1) * nq)
        o_ref[0, rows, lane0:lane0 + LANES] = jnp.where(lo, r[0:nq], r[nq:2 * nq]).astype(BF16)
        o_ref[0, rows, lane0 + LANES:lane0 + 2 * LANES] = (
            jnp.where(lo, r[2 * nq:3 * nq], r[3 * nq:4 * nq]).astype(BF16))

    return lhs_block, finish, k_tiles, v_tiles


def _diff_lambda(lam_ref, lam_init):
    lf = lam_ref[...]
    return (jnp.exp(jnp.sum(lf[0:1] * lf[1:2], axis=-1, keepdims=True))
            - jnp.exp(jnp.sum(lf[2:3] * lf[3:4], axis=-1, keepdims=True)) + lam_init)


def _diff_group(q_ref, o_ref, lane0, k_tiles, v_tiles, lam, gain):
    nq = QUERY_BLOCK
    lane = lax.broadcasted_iota(jnp.int32, (nq, LANES), 1)
    lo = lane < DIFF_V_DIM
    zero = jnp.zeros((nq, LANES), BF16)

    def lhs_block(qb):
        q = q_ref[0, qb * nq:(qb + 1) * nq, lane0:lane0 + LANES]
        return jnp.concatenate(
            [jnp.where((lane >= DIFF_QK_DIM * u) & (lane < DIFF_QK_DIM * (u + 1)), q, zero) for u in range(4)],
            axis=0)

    def finish(r, qb):
        o = jnp.where(lo, r[0:nq] - lam * r[nq:2 * nq], r[2 * nq:3 * nq] - lam * r[3 * nq:4 * nq])
        sq = o * o
        ms_lo = jnp.sum(jnp.where(lo, sq, 0.0), axis=-1, keepdims=True) * (1.0 / DIFF_V_DIM)
        ms_hi = jnp.sum(jnp.where(lo, 0.0, sq), axis=-1, keepdims=True) * (1.0 / DIFF_V_DIM)
        inv = jnp.where(lo, lax.rsqrt(ms_lo + NORM_EPS), lax.rsqrt(ms_hi + NORM_EPS))
        o_ref[0, qb * nq:(qb + 1) * nq, lane0:lane0 + LANES] = ((o * inv) * gain).astype(BF16)

    return lhs_block, finish, k_tiles, v_tiles


def _query_norm_bound(gain, dim):
    return dim * 1.02 * jnp.max(gain * gain, axis=(1, 2))


_SMEM_SPEC = pl.BlockSpec(memory_space=pltpu.SMEM)


def _attn_scratch(tq):
    return [pltpu.VMEM((4 * tq, LANES), F32), pltpu.VMEM((4 * tq, LANES), BF16)]


def _kept_safe_flag(k_tiles, qb2, safe_ref):
    @pl.when(pl.program_id(2) == 0)
    def _():
        safe_ref[0] = _scores_are_safe(k_tiles, qb2).astype(jnp.int32)
    return safe_ref[0] == 1


def _latent_cparams():
    return pltpu.CompilerParams(vmem_limit_bytes=VMEM_LIMIT,
                                dimension_semantics=("arbitrary", "arbitrary", "arbitrary"))


def _gqa_kernel(q_ref, kk_ref, vo_ref, kkc_ref, voc_ref, qb2_ref, o_ref, r_ref, lhs_ref, safe_ref, *, layer):
    k_tiles = _key_tiles(kk_ref, (0, 0), kk_ref.shape[2]) + _key_tiles(kkc_ref, (0, 0, 0), kkc_ref.shape[3])
    v_tiles = _key_tiles(vo_ref, (0, 0), vo_ref.shape[2]) + _key_tiles(voc_ref, (0, 0, 0), voc_ref.shape[3])
    safe = _kept_safe_flag(k_tiles, qb2_ref[layer], safe_ref)
    _attention(q_ref.shape[1], [_gqa_group(q_ref, o_ref, 0, k_tiles, v_tiles)], [safe], r_ref, lhs_ref)


def _gqa_call(qg, kk, vog, cache, layer, qb2, tq):
    b, t, _ = qg.shape
    kkc, voc = cache
    s = kkc.shape[3]
    return pl.pallas_call(
        functools.partial(_gqa_kernel, layer=layer),
        out_shape=jax.ShapeDtypeStruct((b, t, GQA_WIDTH), BF16),
        grid=(b, GQA_KV_HEADS, t // tq),
        in_specs=[
            pl.BlockSpec((1, tq, 256), lambda i, h, j: (i, j, h)),
            pl.BlockSpec((1, 1, t, LANES), lambda i, h, j: (i, h, 0, 0)),
            pl.BlockSpec((1, 1, t, 2 * LANES), lambda i, h, j: (i, h, 0, 0)),
            pl.BlockSpec((1, 1, 1, s, LANES), lambda i, h, j: (i, layer, h, 0, 0)),
            pl.BlockSpec((1, 1, 1, s, 2 * LANES), lambda i, h, j: (i, layer, h, 0, 0)),
            _SMEM_SPEC,
        ],
        out_specs=pl.BlockSpec((1, tq, 256), lambda i, h, j: (i, j, h)),
        scratch_shapes=_attn_scratch(tq) + [pltpu.SMEM((1,), jnp.int32)],
        compiler_params=_latent_cparams(),
        name="gqa_attention",
    )(qg, kk, vog, kkc, voc, qb2)


def _diff_kernel(q_ref, k_ref, vo_ref, kc_ref, voc_ref, lam_ref, sg_ref, qb2_ref, o_ref, r_ref, lhs_ref, safe_ref,
                 *, layer, lam_init):
    k_tiles = _key_tiles(k_ref, (0,), k_ref.shape[1]) + _key_tiles(kc_ref, (0, 0), kc_ref.shape[2])
    v_tiles = _key_tiles(vo_ref, (0, 0), vo_ref.shape[2]) + _key_tiles(voc_ref, (0, 0, 0), voc_ref.shape[3])
    safe = _kept_safe_flag(k_tiles, qb2_ref[layer], safe_ref)
    group = _diff_group(q_ref, o_ref, 0, k_tiles, v_tiles, _diff_lambda(lam_ref, lam_init),
                        sg_ref[...] * (1.0 - lam_init))
    _attention(q_ref.shape[1], [group], [safe], r_ref, lhs_ref)


def _diff_call(qd, kd, vod, cache, layer, lam_p, subg, qb2, lam_init, tq):
    b, t, _ = qd.shape
    kdc, vodc = cache
    s = kdc.shape[2]
    return pl.pallas_call(
        functools.partial(_diff_kernel, layer=layer, lam_init=lam_init),
        out_shape=jax.ShapeDtypeStruct((b, t, DIFF_WIDTH), BF16),
        grid=(b, 2, t // tq),
        in_specs=[
            pl.BlockSpec((1, tq, LANES), lambda i, h, j: (i, j, h)),
            pl.BlockSpec((1, t, LANES), lambda i, h, j: (i, 0, h)),
            pl.BlockSpec((1, 1, t, 2 * LANES), lambda i, h, j: (i, h, 0, 0)),
            pl.BlockSpec((1, 1, s, LANES), lambda i, h, j: (i, layer, 0, h)),
            pl.BlockSpec((1, 1, 1, s, 2 * LANES), lambda i, h, j: (i, layer, h, 0, 0)),
            _layer_spec((4, DIFF_QK_DIM), layer), _layer_spec((1, LANES), layer), _SMEM_SPEC,
        ],
        out_specs=pl.BlockSpec((1, tq, LANES), lambda i, h, j: (i, j, h)),
        scratch_shapes=_attn_scratch(tq) + [pltpu.SMEM((1,), jnp.int32)],
        compiler_params=_latent_cparams(),
        name="diff_attention",
    )(qd, kd, vod, kdc, vodc, lam_p, subg, qb2)


def _ctx_attn_kernel(qg_ref, kk_ref, vog_ref, qd_ref, kd_ref, vod_ref, lam_ref, sg_ref, qb2g_ref, qb2d_ref,
                     og_ref, od_ref, r_ref, lhs_ref, *, layer, lam_init):
    t = qg_ref.shape[1]
    lam = _diff_lambda(lam_ref, lam_init)
    gain = sg_ref[...] * (1.0 - lam_init)
    groups, safes = [], []
    for kvh in range(GQA_KV_HEADS):
        k_tiles, v_tiles = _key_tiles(kk_ref, (0, kvh), t), _key_tiles(vog_ref, (0, kvh), t)
        groups.append(_gqa_group(qg_ref, og_ref, 2 * LANES * kvh, k_tiles, v_tiles))
        safes.append(_scores_are_safe(k_tiles, qb2g_ref[layer]))
    for pair in range(DIFF_HEADS // 2):
        lanes = slice(LANES * pair, LANES * (pair + 1))
        k_tiles, v_tiles = _key_tiles(kd_ref, (0,), t, lanes), _key_tiles(vod_ref, (0, pair), t)
        groups.append(_diff_group(qd_ref, od_ref, LANES * pair, k_tiles, v_tiles, lam, gain))
        safes.append(_scores_are_safe(k_tiles, qb2d_ref[layer]))
    _attention(t, groups, safes, r_ref, lhs_ref)


def _ctx_attn_call(qg, kk, vog, qd, kd, vod, layer, lam_p, subg, qb2_g, qb2_d, lam_init):
    b, t, _ = qg.shape
    tok = lambda w: pl.BlockSpec((1, t, w), lambda i: (i, 0, 0))
    pair = lambda w: pl.BlockSpec((1, 2, t, w), lambda i: (i, 0, 0, 0))
    return pl.pallas_call(
        functools.partial(_ctx_attn_kernel, layer=layer, lam_init=lam_init),
        out_shape=(jax.ShapeDtypeStruct((b, t, GQA_WIDTH), BF16), jax.ShapeDtypeStruct((b, t, DIFF_WIDTH), BF16)),
        grid=(b,),
        in_specs=[tok(GQA_WIDTH), pair(LANES), pair(2 * LANES), tok(DIFF_WIDTH), tok(DIFF_WIDTH), pair(2 * LANES),
                  _layer_spec((4, DIFF_QK_DIM), layer), _layer_spec((1, LANES), layer), _SMEM_SPEC, _SMEM_SPEC],
        out_specs=(tok(GQA_WIDTH), tok(DIFF_WIDTH)),
        scratch_shapes=_attn_scratch(t),
        compiler_params=_cparams(),
        name="context_attention",
    )(qg, kk, vog, qd, kd, vod, lam_p, subg, qb2_g, qb2_d)


def _post_kernel(*refs, halo, tm, t_len):
    it = iter(refs)
    n_blk = 3 if halo else 1

    def take():
        blocks = [next(it) for _ in range(n_blk)]
        if halo:
            return jnp.concatenate([blocks[0][0], blocks[1][0], blocks[2][0]], axis=0)
        return blocks[0][0]

    xe, og, od, cb, ccu = take(), take(), take(), take(), take()
    mod_ref, cw_ref, cbias_ref, wout_ref, n2g_ref = (next(it) for _ in range(5))
    up_ref, fcw_ref, fcb_ref, down_ref = (next(it) for _ in range(4))
    out_ref, ccu_scr, a_scr = next(it), next(it), next(it)

    h0 = HALO if halo else 0
    rows = tm + 2 * h0
    mods = mod_ref[0]
    g1, sh2, sc2, g2 = mods[2:3], mods[3:4], mods[4:5], mods[5:6]

    if halo:
        pos = pl.program_id(1) * tm - h0 + lax.broadcasted_iota(jnp.int32, (rows, 1), 0)
        valid = (pos >= 0) & (pos < t_len)
        ccu = jnp.where(valid, ccu, 0.0)

    ccu_scr[0:PAD] = jnp.zeros((PAD, CONV_WIDTH), F32)
    ccu_scr[PAD + rows: 2 * PAD + rows] = jnp.zeros((PAD, CONV_WIDTH), F32)
    ccu_scr[PAD: PAD + rows] = ccu
    cw = cw_ref[...]
    conv = (ccu_scr[PAD - 1: PAD - 1 + rows] * cw[0:1] + ccu * cw[1:2]
            + ccu_scr[PAD + 1: PAD + 1 + rows] * cw[2:3] + cbias_ref[...])
    oc = (cb * conv).astype(BF16)

    y = (jnp.dot(og, wout_ref[0:GQA_WIDTH], preferred_element_type=F32)
         + jnp.dot(oc, wout_ref[GQA_WIDTH: GQA_WIDTH + CONV_WIDTH], preferred_element_type=F32)
         + jnp.dot(od, wout_ref[GQA_WIDTH + CONV_WIDTH:], preferred_element_type=F32))
    xmid = xe + g1 * y

    inv = lax.rsqrt(jnp.mean(xmid * xmid, axis=-1, keepdims=True) + NORM_EPS)
    h2 = ((xmid * inv) * n2g_ref[...] * (1.0 + sc2) + sh2).astype(BF16)
    a = jnp.dot(h2, up_ref[:, :D_FF], preferred_element_type=F32)
    if halo:
        a = jnp.where(valid, a, 0.0)
    a_scr[0:PAD] = jnp.zeros((PAD, D_FF), F32)
    a_scr[PAD + rows: 2 * PAD + rows] = jnp.zeros((PAD, D_FF), F32)
    a_scr[PAD: PAD + rows] = a
    base = PAD + h0
    fcw = fcw_ref[...]
    ac = (a_scr[base - 1: base - 1 + tm] * fcw[0:1] + a_scr[base: base + tm] * fcw[1:2]
          + a_scr[base + 1: base + 1 + tm] * fcw[2:3] + fcb_ref[...])
    u = jnp.dot(h2[h0: h0 + tm], up_ref[:, D_FF:], preferred_element_type=F32)
    f = (ac * jax.nn.sigmoid(ac) * u).astype(BF16)
    out_ref[0] = xmid[h0: h0 + tm] + g2 * jnp.dot(f, down_ref[...], preferred_element_type=F32)


def _post_call(x, og, od, cb, ccu, layer, mods, mod_row, cw, cbias, w_out, n2g, up, fcw, fcb, down, tm):
    b, t, d = x.shape
    nt = t // tm
    halo = nt > 1
    per = tm // HALO
    last = t // HALO - 1

    def specs(width):
        main = pl.BlockSpec((1, tm, width), lambda i, j: (i, j, 0))
        if not halo:
            return [main]
        prev = pl.BlockSpec((1, HALO, width), lambda i, j: (i, jnp.maximum(j * per - 1, 0), 0))
        nxt = pl.BlockSpec((1, HALO, width), lambda i, j: (i, jnp.minimum((j + 1) * per, last), 0))
        return [prev, main, nxt]

    in_specs, args = [], []
    for arr in (x, og, od, cb, ccu):
        sp = specs(arr.shape[-1])
        in_specs += sp
        args += [arr] * len(sp)
    in_specs += [
        _mods_spec(layer, mod_row),
        _layer_spec((3, CONV_WIDTH), layer), _layer_spec((1, CONV_WIDTH), layer),
        _layer_spec((d, d), layer), _layer_spec((1, d), layer),
        _layer_spec((d, 2 * D_FF), layer), _layer_spec((3, D_FF), layer), _layer_spec((1, D_FF), layer),
        _layer_spec((D_FF, d), layer),
    ]
    args += [mods, cw, cbias, w_out, n2g, up, fcw, fcb, down]
    rows = tm + (2 * HALO if halo else 0)
    return pl.pallas_call(
        functools.partial(_post_kernel, halo=halo, tm=tm, t_len=t),
        out_shape=jax.ShapeDtypeStruct((b, t, d), F32),
        grid=(b, nt),
        in_specs=in_specs,
        out_specs=pl.BlockSpec((1, tm, d), lambda i, j: (i, j, 0)),
        scratch_shapes=[pltpu.VMEM((rows + 2 * PAD, CONV_WIDTH), F32),
                        pltpu.VMEM((rows + 2 * PAD, D_FF), F32)],
        compiler_params=_cparams(),
        name="post_attention",
    )(*args)


def _rope_tables(n_rows, dim):
    row = jnp.repeat(jnp.arange(n_rows), GRID_W).astype(F32)
    col = jnp.tile(jnp.arange(GRID_W), n_rows).astype(F32)
    nf = dim // 4
    freqs = ROPE_THETA ** (-jnp.arange(nf, dtype=F32) / nf)
    ar = row[:, None] * freqs[None, :]
    ac = col[:, None] * freqs[None, :]
    ang = jnp.concatenate([ar, ar, ac, ac], axis=-1)
    cos, sin = jnp.cos(ang), jnp.sin(ang)
    first = ((jnp.arange(dim) // nf) % 2 == 0)[None, :]
    sin_up = jnp.where(first, -sin, 0.0)
    sin_dn = jnp.where(first, 0.0, sin)
    rep = LANES // dim
    return tuple(jnp.tile(tab, (1, rep)) for tab in (cos, sin_up, sin_dn))


def _group_matrix(group):
    idx = jnp.arange(256) // group
    return (idx[:, None] == idx[None, :]).astype(BF16)


def kernel(x_prompt, x_sample, cache_gqa_k, cache_gqa_v, cache_diff_k, cache_diff_v, c, c_ctx,
           w_mod, b_mod, norm1_g, w_in, gqa_qn_g, gqa_kn_g, conv_w, conv_b, diff_qn_g, diff_kn_g,
           diff_lambda, diff_subln_g, w_out, norm2_g, ffn_up, ffn_conv_w, ffn_conv_b, ffn_down):
    depth = w_in.shape[0]
    batch, seq, _ = x_prompt.shape
    dec_batch, dec_seq, _ = x_sample.shape

    cond_all = jnp.zeros((MOD_ROWS, D_MODEL), F32).at[0].set(c_ctx).at[1:1 + dec_batch].set(c)
    mods = _mods_call(cond_all, w_mod, b_mod)
    kkc, vogc, kdc, vodc = _cache_prep_call(cache_gqa_k, cache_gqa_v, cache_diff_k, cache_diff_v)

    w_in_b = w_in.astype(BF16)
    w_out_b = w_out.astype(BF16)
    up_b = ffn_up.astype(BF16)
    down_b = ffn_down.astype(BF16)
    g64, g32 = _group_matrix(HEAD_DIM), _group_matrix(DIFF_QK_DIM)
    rope_tabs = _rope_tables(dec_seq // GRID_W, HEAD_DIM) + _rope_tables(dec_seq // GRID_W, DIFF_QK_DIM)

    row = lambda p: p.reshape(depth, 1, -1)
    lane_tiled = lambda p, rep: jnp.tile(p, (1, rep)).reshape(depth, 1, LANES)
    gq = lane_tiled(gqa_qn_g * (HEAD_DIM ** -0.5 * LOG2E), 2)
    gk = lane_tiled(gqa_kn_g, 2)
    gqd = lane_tiled(diff_qn_g * (DIFF_QK_DIM ** -0.5 * LOG2E), 4)
    gkd = lane_tiled(diff_kn_g, 4)
    subg = lane_tiled(diff_subln_g, 2)
    qb2_g = _query_norm_bound(gq, HEAD_DIM)
    qb2_d = _query_norm_bound(gqd, DIFF_QK_DIM)
    n1g, n2g, cbias, fcb = row(norm1_g), row(norm2_g), row(conv_b), row(ffn_conv_b)

    def layer(x, l, mod_row, rope, cache_g, cache_d, own, tp, tm, tq):
        lam_init = 0.8 - 0.6 * math.exp(-0.3 * l)
        pre = _pre_call(x, l, mods, mod_row, n1g, w_in_b, gq, gk, gqd, gkd, g64, g32, rope, own, tp)
        qg, kk, vog, qd, kd, vod, cb, ccu = pre[:8]
        if cache_g is None:
            og, od = _ctx_attn_call(qg, kk, vog, qd, kd, vod, l, diff_lambda, subg, qb2_g, qb2_d, lam_init)
        else:
            og = _gqa_call(qg, kk, vog, cache_g, l, qb2_g, tq)
            od = _diff_call(qd, kd, vod, cache_d, l, diff_lambda, subg, qb2_d, lam_init, tq)
        x = _post_call(x, og, od, cb, ccu, l, mods, mod_row, conv_w, cbias, w_out_b, n2g, up_b,
                       ffn_conv_w, fcb, down_b, tm)
        return x, pre[8:]

    xp = x_prompt
    owns = []
    for l in range(depth):
        xp, own = layer(xp, l, lambda i: 0, None, None, None, True, seq, seq, seq)
        owns.append(own)
    new_gqa_k = jnp.stack([o[0] for o in owns], axis=1).reshape(batch, depth, seq, GQA_KV_HEADS, HEAD_DIM)
    new_gqa_v = jnp.stack([o[1] for o in owns], axis=1).reshape(batch, depth, seq, GQA_KV_HEADS, HEAD_DIM)
    new_diff_k = jnp.stack([o[2] for o in owns], axis=1).reshape(batch, depth, seq, DIFF_HEADS, 2, DIFF_QK_DIM)
    new_diff_v = jnp.stack([o[3] for o in owns], axis=1).reshape(batch, depth, seq, DIFF_HEADS, DIFF_V_DIM)

    xs = x_sample
    for l in range(depth):
        xs, _ = layer(xs, l, lambda i: i + 1, rope_tabs, (kkc, vogc), (kdc, vodc), False, 1024, 512, 512)

    return (xp, xs, new_gqa_k, new_gqa_v, new_diff_k, new_diff_v)
```

```python
import functools
import math

import jax
import jax.numpy as jnp
from jax import lax
from jax.experimental import pallas as pl
from jax.experimental.pallas import tpu as pltpu

F32 = jnp.float32
BF16 = jnp.bfloat16

D_MODEL = 1024
GRID_W = 64
HEAD_DIM = 64
GQA_HEADS = 8
GQA_KV_HEADS = 2
GQA_WIDTH = GQA_HEADS * HEAD_DIM
CONV_WIDTH = 256
DIFF_HEADS = 4
DIFF_QK_DIM = 32
DIFF_V_DIM = 64
DIFF_WIDTH = DIFF_HEADS * DIFF_V_DIM
IN_WIDTH = 2304
D_FF = 2816
ROPE_THETA = 10000.0
NORM_EPS = 1e-6
N_MOD = 6
LANES = 128
MOD_ROWS = 16

OFF_QG, OFF_KG, OFF_VG = 0, 512, 640
OFF_CB, OFF_CC, OFF_CU = 768, 1024, 1280
OFF_QD, OFF_KD, OFF_VD = 1536, 1792, 2048

HALO = 16
PAD = 8

VMEM_LIMIT = 60 * 1024 * 1024

KEY_TILE = 256
SAFE_EXPONENT = 60.0
LOG2E = math.log2(math.e)
PRE_ROWS = 128
QUERY_BLOCK = 64
EXACT_ROWS = 512


def _cparams():
    return pltpu.CompilerParams(vmem_limit_bytes=VMEM_LIMIT)


def _const_spec(shape):
    nd = len(shape)
    return pl.BlockSpec(shape, lambda *_: (0,) * nd, pipeline_mode=pl.Buffered(1))


def _layer_spec(shape, layer):
    nd = len(shape)
    return pl.BlockSpec((None,) + tuple(shape), lambda *_: (layer,) + (0,) * nd, pipeline_mode=pl.Buffered(1))


def _mods_spec(layer, mod_row):
    return pl.BlockSpec((None, 1, N_MOD, D_MODEL), lambda i, j: (layer, mod_row(i), 0, 0))


def _mods_kernel(cond_ref, w_ref, b_ref, o_ref):
    cnd = cond_ref[...]
    act = cnd * jax.nn.sigmoid(cnd)
    o_ref[0] = jnp.dot(act, w_ref[0], preferred_element_type=F32,
                       precision=lax.Precision.HIGHEST) + b_ref[0]


def _mods_call(cond_all, w_mod, b_mod):
    depth = w_mod.shape[0]
    width = N_MOD * D_MODEL
    tn = 1024
    out = pl.pallas_call(
        _mods_kernel,
        out_shape=jax.ShapeDtypeStruct((depth, MOD_ROWS, width), F32),
        grid=(depth, width // tn),
        in_specs=[
            pl.BlockSpec((MOD_ROWS, D_MODEL), lambda l, j: (0, 0)),
            pl.BlockSpec((1, D_MODEL, tn), lambda l, j: (l, 0, j)),
            pl.BlockSpec((1, 1, tn), lambda l, j: (l, 0, j)),
        ],
        out_specs=pl.BlockSpec((1, MOD_ROWS, tn), lambda l, j: (l, 0, j)),
        compiler_params=_cparams(),
        name="adaln_mods",
    )(cond_all, w_mod, b_mod.reshape(depth, 1, width))
    return out.reshape(depth, MOD_ROWS, N_MOD, D_MODEL)


def _dup_halves(x128):
    lo = lax.broadcasted_iota(jnp.int32, x128.shape, 1) < HEAD_DIM
    swapped = pltpu.roll(x128, HEAD_DIM, 1)
    return jnp.where(lo, x128, swapped), jnp.where(lo, swapped, x128)


def _cache_prep_kernel(gk_ref, gv_ref, dk_ref, dv_ref, kk_ref, vog_ref, kd_ref, vod_ref):
    ones = jnp.ones((gk_ref.shape[2], LANES), BF16)
    k0, k1 = _dup_halves(gk_ref[0, 0])
    kk_ref[0, 0, 0] = k0.astype(BF16)
    kk_ref[0, 0, 1] = k1.astype(BF16)
    v0, v1 = _dup_halves(gv_ref[0, 0])
    vog_ref[0, 0, 0, :, :LANES] = v0.astype(BF16)
    vog_ref[0, 0, 1, :, :LANES] = v1.astype(BF16)
    vog_ref[0, 0, 0, :, LANES:] = ones
    vog_ref[0, 0, 1, :, LANES:] = ones
    kd_ref[0, 0] = dk_ref[0, 0].astype(BF16)
    dv = dv_ref[0, 0]
    vod_ref[0, 0, 0, :, :LANES] = dv[:, :LANES].astype(BF16)
    vod_ref[0, 0, 1, :, :LANES] = dv[:, LANES:].astype(BF16)
    vod_ref[0, 0, 0, :, LANES:] = ones
    vod_ref[0, 0, 1, :, LANES:] = ones


def _cache_prep_call(cache_gqa_k, cache_gqa_v, cache_diff_k, cache_diff_v):
    b, depth, s = cache_gqa_k.shape[:3]
    gk = cache_gqa_k.reshape(b, depth, s, LANES)
    gv = cache_gqa_v.reshape(b, depth, s, LANES)
    dk = cache_diff_k.reshape(b, depth, s, 2 * LANES)
    dv = cache_diff_v.reshape(b, depth, s, 2 * LANES)
    idx4 = lambda i, l: (i, l, 0, 0)
    idx5 = lambda i, l: (i, l, 0, 0, 0)
    return pl.pallas_call(
        _cache_prep_kernel,
        out_shape=(
            jax.ShapeDtypeStruct((b, depth, 2, s, LANES), BF16),
            jax.ShapeDtypeStruct((b, depth, 2, s, 2 * LANES), BF16),
            jax.ShapeDtypeStruct((b, depth, s, 2 * LANES), BF16),
            jax.ShapeDtypeStruct((b, depth, 2, s, 2 * LANES), BF16),
        ),
        grid=(b, depth),
        in_specs=[
            pl.BlockSpec((1, 1, s, LANES), idx4),
            pl.BlockSpec((1, 1, s, LANES), idx4),
            pl.BlockSpec((1, 1, s, 2 * LANES), idx4),
            pl.BlockSpec((1, 1, s, 2 * LANES), idx4),
        ],
        out_specs=(
            pl.BlockSpec((1, 1, 2, s, LANES), idx5),
            pl.BlockSpec((1, 1, 2, s, 2 * LANES), idx5),
            pl.BlockSpec((1, 1, s, 2 * LANES), idx4),
            pl.BlockSpec((1, 1, 2, s, 2 * LANES), idx5),
        ),
        compiler_params=_cparams(),
        name="cache_prep",
    )(gk, gv, dk, dv)


def _group_inv_rms(z, g_ref, group):
    sq = z * z
    hi = sq.astype(BF16)
    lo = (sq - hi.astype(F32)).astype(BF16)
    width = z.shape[1]
    gmat = g_ref[:width, :width]
    ssum = (jnp.dot(hi, gmat, preferred_element_type=F32)
            + jnp.dot(lo, gmat, preferred_element_type=F32))
    return lax.rsqrt(ssum * (1.0 / group) + NORM_EPS)


def _rope_chunk(y, cos, sin_up, sin_dn, quarter):
    up = pltpu.roll(y, LANES - quarter, 1)
    dn = pltpu.roll(y, quarter, 1)
    return y * cos + up * sin_up + dn * sin_dn


def _pre_kernel(*refs, rope, own):
    it = iter(refs)
    x_ref, mod_ref, n1g_ref, w_in_ref = next(it), next(it), next(it), next(it)
    gq_ref, gk_ref, gqd_ref, gkd_ref = next(it), next(it), next(it), next(it)
    g64_ref, g32_ref = next(it), next(it)
    if rope:
        c64, su64, sd64, c32, su32, sd32 = (next(it) for _ in range(6))
    qg_ref, kk_ref, vog_ref, qd_ref, kd_ref, vod_ref, cb_ref, ccu_ref = (next(it) for _ in range(8))
    if own:
        kown_ref, vown_ref, kdown_ref, vdown_ref = (next(it) for _ in range(4))

    mods = mod_ref[0]
    sh1, sc1 = mods[0:1], mods[1:2]
    gain1 = n1g_ref[...] * (1.0 + sc1)
    ones = jnp.ones((PRE_ROWS, LANES), BF16)

    for r in range(x_ref.shape[1] // PRE_ROWS):
        rs = slice(r * PRE_ROWS, (r + 1) * PRE_ROWS)
        x = x_ref[0, rs]
        inv = lax.rsqrt(jnp.mean(x * x, axis=-1, keepdims=True) + NORM_EPS)
        h = (x * inv) * gain1 + sh1
        z = jnp.dot(h.astype(BF16), w_in_ref[...], preferred_element_type=F32)

        def rope64(y):
            if not rope:
                return y
            return _rope_chunk(y, c64[rs, :], su64[rs, :], sd64[rs, :], HEAD_DIM // 4)

        def rope32(y):
            if not rope:
                return y
            return _rope_chunk(y, c32[rs, :], su32[rs, :], sd32[rs, :], DIFF_QK_DIM // 4)

        for c in range(2):
            zc = z[:, OFF_QG + 256 * c: OFF_QG + 256 * (c + 1)]
            y = zc * _group_inv_rms(zc, g64_ref, HEAD_DIM)
            for j in range(2):
                yj = rope64(y[:, LANES * j: LANES * (j + 1)] * gq_ref[...])
                qg_ref[0, rs, 256 * c + LANES * j: 256 * c + LANES * (j + 1)] = yj.astype(BF16)

        zk = z[:, OFF_KG: OFF_KG + LANES]
        zv = z[:, OFF_VG: OFF_VG + LANES]
        kn = zk * _group_inv_rms(zk, g64_ref, HEAD_DIM) * gk_ref[...]
        if own:
            kown_ref[0, rs] = kn
            vown_ref[0, rs] = zv
        k0, k1 = _dup_halves(rope64(kn))
        kk_ref[0, 0, rs] = k0.astype(BF16)
        kk_ref[0, 1, rs] = k1.astype(BF16)
        v0, v1 = _dup_halves(zv)
        vog_ref[0, 0, rs, :LANES] = v0.astype(BF16)
        vog_ref[0, 1, rs, :LANES] = v1.astype(BF16)
        vog_ref[0, 0, rs, LANES:] = ones
        vog_ref[0, 1, rs, LANES:] = ones

        cb_ref[0, rs] = z[:, OFF_CB: OFF_CB + CONV_WIDTH]
        ccu_ref[0, rs] = z[:, OFF_CC: OFF_CC + CONV_WIDTH] * z[:, OFF_CU: OFF_CU + CONV_WIDTH]

        zq = z[:, OFF_QD: OFF_QD + 256]
        yq = zq * _group_inv_rms(zq, g32_ref, DIFF_QK_DIM)
        zkd = z[:, OFF_KD: OFF_KD + 256]
        ykd = zkd * _group_inv_rms(zkd, g32_ref, DIFF_QK_DIM)
        zvd = z[:, OFF_VD: OFF_VD + 256]
        for j in range(2):
            sl = slice(LANES * j, LANES * (j + 1))
            qd_ref[0, rs, sl] = rope32(yq[:, sl] * gqd_ref[...]).astype(BF16)
            kdj = ykd[:, sl] * gkd_ref[...]
            if own:
                kdown_ref[0, rs, sl] = kdj
            kd_ref[0, rs, sl] = rope32(kdj).astype(BF16)
            vod_ref[0, j, rs, :LANES] = zvd[:, sl].astype(BF16)
            vod_ref[0, j, rs, LANES:] = ones
        if own:
            vdown_ref[0, rs] = zvd


def _pre_call(x, layer, mods, mod_row, n1g, w_in, gq, gk, gqd, gkd, g64, g32, rope_tabs, own, tm):
    b, t, d = x.shape
    nt = t // tm
    rope = rope_tabs is not None
    gain_spec = _layer_spec((1, LANES), layer)
    in_specs = [
        pl.BlockSpec((1, tm, d), lambda i, j: (i, j, 0)),
        _mods_spec(layer, mod_row),
        _layer_spec((1, d), layer),
        _layer_spec((d, IN_WIDTH), layer),
        gain_spec, gain_spec, gain_spec, gain_spec,
        _const_spec((256, 256)), _const_spec((256, 256)),
    ]
    args = [x, mods, n1g, w_in, gq, gk, gqd, gkd, g64, g32]
    if rope:
        in_specs += [pl.BlockSpec((tm, LANES), lambda i, j: (j, 0))] * 6
        args += list(rope_tabs)
    tok = lambda w, dt: jax.ShapeDtypeStruct((b, t, w), dt)
    tok_spec = lambda w: pl.BlockSpec((1, tm, w), lambda i, j: (i, j, 0))
    pair = lambda w: jax.ShapeDtypeStruct((b, 2, t, w), BF16)
    pair_spec = lambda w: pl.BlockSpec((1, 2, tm, w), lambda i, j: (i, 0, j, 0))
    out_shape = [tok(GQA_WIDTH, BF16), pair(LANES), pair(2 * LANES), tok(256, BF16), tok(256, BF16),
                 pair(2 * LANES), tok(CONV_WIDTH, F32), tok(CONV_WIDTH, F32)]
    out_specs = [tok_spec(GQA_WIDTH), pair_spec(LANES), pair_spec(2 * LANES), tok_spec(256), tok_spec(256),
                 pair_spec(2 * LANES), tok_spec(CONV_WIDTH), tok_spec(CONV_WIDTH)]
    if own:
        out_shape += [tok(LANES, F32), tok(LANES, F32), tok(256, F32), tok(256, F32)]
        out_specs += [tok_spec(LANES), tok_spec(LANES), tok_spec(256), tok_spec(256)]
    return pl.pallas_call(
        functools.partial(_pre_kernel, rope=rope, own=own),
        out_shape=tuple(out_shape),
        grid=(b, nt),
        in_specs=in_specs,
        out_specs=tuple(out_specs),
        compiler_params=_cparams(),
        name="pre_attention",
    )(*args)


def _key_tiles(ref, lead, n_rows, lanes=slice(None)):
    def load(t):
        return ref[lead + (pl.ds(t * KEY_TILE, KEY_TILE), lanes)]
    return [functools.partial(load, t) for t in range(n_rows // KEY_TILE)]


def _row_norm2_max(x):
    xf = x.astype(F32)
    return jnp.max(jnp.sum(xf * xf, axis=-1, keepdims=True))


def _dot_nt(a, b):
    return lax.dot_general(a, b, (((1,), (1,)), ((), ())), preferred_element_type=F32)


def _scores_are_safe(k_tiles, qb2):
    kmax2 = functools.reduce(jnp.maximum, [_row_norm2_max(k()) for k in k_tiles])
    return qb2 * kmax2 <= SAFE_EXPONENT ** 2


def _softmax_pv_shifted(lhs_ref, k_tiles, v_tiles, r_ref):
    def row_block(rb, carry):
        rows = pl.ds(pl.multiple_of(rb * EXACT_ROWS, EXACT_ROWS), EXACT_ROWS)
        lhs_b = lhs_ref[rows, :]
        scores = [_dot_nt(lhs_b, k()) for k in k_tiles]
        m = functools.reduce(jnp.maximum, [jnp.max(s, axis=-1, keepdims=True) for s in scores])
        acc = None
        for s, v in zip(scores, v_tiles):
            p = jnp.exp2(s - m).astype(BF16)
            part = jnp.dot(p, v(), preferred_element_type=F32)
            acc = part if acc is None else acc + part
        r_ref[rows, :] = acc[:, :LANES] / acc[:, LANES:]
        return carry

    lax.fori_loop(0, lhs_ref.shape[0] // EXACT_ROWS, row_block, 0)


def _group_unshifted(tq, group):
    lhs_block, finish, k_tiles, v_tiles = group
    for qb in range(tq // QUERY_BLOCK):
        lhs_b = lhs_block(qb)
        acc = None
        for k, v in zip(k_tiles, v_tiles):
            p = jnp.exp2(_dot_nt(lhs_b, k())).astype(BF16)
            part = jnp.dot(p, v(), preferred_element_type=F32)
            acc = part if acc is None else acc + part
        finish(acc[:, :LANES] / acc[:, LANES:], qb)


def _group_shifted(tq, group, r_ref, lhs_ref):
    lhs_block, finish, k_tiles, v_tiles = group
    rows = 4 * QUERY_BLOCK
    for qb in range(tq // QUERY_BLOCK):
        lhs_ref[qb * rows:(qb + 1) * rows, :] = lhs_block(qb)
    _softmax_pv_shifted(lhs_ref, k_tiles, v_tiles, r_ref)
    for qb in range(tq // QUERY_BLOCK):
        finish(r_ref[qb * rows:(qb + 1) * rows, :], qb)


def _attention(tq, groups, safes, r_ref, lhs_ref):
    all_safe = functools.reduce(jnp.logical_and, safes)

    @pl.when(all_safe)
    def _():
        for group in groups:
            _group_unshifted(tq, group)

    @pl.when(jnp.logical_not(all_safe))
    def _():
        for group, safe in zip(groups, safes):
            if len(groups) > 1:
                pl.when(safe)(functools.partial(_group_unshifted, tq, group))
            pl.when(jnp.logical_not(safe))(functools.partial(_group_shifted, tq, group, r_ref, lhs_ref))


def _gqa_group(q_ref, o_ref, lane0, k_tiles, v_tiles):
    nq = QUERY_BLOCK
    lo = lax.broadcasted_iota(jnp.int32, (nq, LANES), 1) < HEAD_DIM
    zero = jnp.zeros((nq, LANES), BF16)

    def lhs_block(qb):
        rows = slice(qb * nq, (qb + 1) * nq)
        qa = q_ref[0, rows, lane0:lane0 + LANES]
        qc = q_ref[0, rows, lane0 + LANES:lane0 + 2 * LANES]
        return jnp.concatenate([jnp.where(lo, qa, zero), jnp.where(lo, zero, qa),
                                jnp.where(lo, qc, zero), jnp.where(lo, zero, qc)], axis=0)

    def finish(r, qb):
        rows = slice(qb * nq, (qb + 1) * nq)
        o_ref[0, rows, lane0:lane0 + LANES] = jnp.where(lo, r[0:nq], r[nq:2 * nq]).astype(BF16)
        o_ref[0, rows, lane0 + LANES:lane0 + 2 * LANES] = (
            jnp.where(lo, r[2 * nq:3 * nq], r[3 * nq:4 * nq]).astype(BF16))

    return lhs_block, finish, k_tiles, v_tiles


def _diff_lambda(lam_ref, lam_init):
    lf = lam_ref[...]
    return (jnp.exp(jnp.sum(lf[0:1] * lf[1:2], axis=-1, keepdims=True))
            - jnp.exp(jnp.sum(lf[2:3] * lf[3:4], axis=-1, keepdims=True)) + lam_init)


def _diff_group(q_ref, o_ref, lane0, k_tiles, v_tiles, lam, gain):
    nq = QUERY_BLOCK
    lane = lax.broadcasted_iota(jnp.int32, (nq, LANES), 1)
    lo = lane < DIFF_V_DIM
    zero = jnp.zeros((nq, LANES), BF16)

    def lhs_block(qb):
        q = q_ref[0, qb * nq:(qb + 1) * nq, lane0:lane0 + LANES]
        return jnp.concatenate(
            [jnp.where((lane >= DIFF_QK_DIM * u) & (lane < DIFF_QK_DIM * (u + 1)), q, zero) for u in range(4)],
            axis=0)

    def finish(r, qb):
        o = jnp.where(lo, r[0:nq] - lam * r[nq:2 * nq], r[2 * nq:3 * nq] - lam * r[3 * nq:4 * nq])
        sq = o * o
        ms_lo = jnp.sum(jnp.where(lo, sq, 0.0), axis=-1, keepdims=True) * (1.0 / DIFF_V_DIM)
        ms_hi = jnp.sum(jnp.where(lo, 0.0, sq), axis=-1, keepdims=True) * (1.0 / DIFF_V_DIM)
        inv = jnp.where(lo, lax.rsqrt(ms_lo + NORM_EPS), lax.rsqrt(ms_hi + NORM_EPS))
        o_ref[0, qb * nq:(qb + 1) * nq, lane0:lane0 + LANES] = ((o * inv) * gain).astype(BF16)

    return lhs_block, finish, k_tiles, v_tiles


def _query_norm_bound(gain, dim):
    return dim * 1.02 * jnp.max(gain * gain, axis=(1, 2))


_SMEM_SPEC = pl.BlockSpec(memory_space=pltpu.SMEM)


def _attn_scratch(tq):
    return [pltpu.VMEM((4 * tq, LANES), F32), pltpu.VMEM((4 * tq, LANES), BF16)]


def _kept_safe_flag(k_tiles, qb2, safe_ref):
    @pl.when(pl.program_id(2) == 0)
    def _():
        safe_ref[0] = _scores_are_safe(k_tiles, qb2).astype(jnp.int32)
    return safe_ref[0] == 1


def _latent_cparams():
    return pltpu.CompilerParams(vmem_limit_bytes=VMEM_LIMIT,
                                dimension_semantics=("arbitrary", "arbitrary", "arbitrary"))


def _gqa_kernel(q_ref, kk_ref, vo_ref, kkc_ref, voc_ref, qb2_ref, o_ref, r_ref, lhs_ref, safe_ref, *, layer):
    k_tiles = _key_tiles(kk_ref, (0, 0), kk_ref.shape[2]) + _key_tiles(kkc_ref, (0, 0, 0), kkc_ref.shape[3])
    v_tiles = _key_tiles(vo_ref, (0, 0), vo_ref.shape[2]) + _key_tiles(voc_ref, (0, 0, 0), voc_ref.shape[3])
    safe = _kept_safe_flag(k_tiles, qb2_ref[layer], safe_ref)
    _attention(q_ref.shape[1], [_gqa_group(q_ref, o_ref, 0, k_tiles, v_tiles)], [safe], r_ref, lhs_ref)


def _gqa_call(qg, kk, vog, cache, layer, qb2, tq):
    b, t, _ = qg.shape
    kkc, voc = cache
    s = kkc.shape[3]
    return pl.pallas_call(
        functools.partial(_gqa_kernel, layer=layer),
        out_shape=jax.ShapeDtypeStruct((b, t, GQA_WIDTH), BF16),
        grid=(b, GQA_KV_HEADS, t // tq),
        in_specs=[
            pl.BlockSpec((1, tq, 256), lambda i, h, j: (i, j, h)),
            pl.BlockSpec((1, 1, t, LANES), lambda i, h, j: (i, h, 0, 0)),
            pl.BlockSpec((1, 1, t, 2 * LANES), lambda i, h, j: (i, h, 0, 0)),
            pl.BlockSpec((1, 1, 1, s, LANES), lambda i, h, j: (i, layer, h, 0, 0)),
            pl.BlockSpec((1, 1, 1, s, 2 * LANES), lambda i, h, j: (i, layer, h, 0, 0)),
            _SMEM_SPEC,
        ],
        out_specs=pl.BlockSpec((1, tq, 256), lambda i, h, j: (i, j, h)),
        scratch_shapes=_attn_scratch(tq) + [pltpu.SMEM((1,), jnp.int32)],
        compiler_params=_latent_cparams(),
        name="gqa_attention",
    )(qg, kk, vog, kkc, voc, qb2)


def _diff_kernel(q_ref, k_ref, vo_ref, kc_ref, voc_ref, lam_ref, sg_ref, qb2_ref, o_ref, r_ref, lhs_ref, safe_ref,
                 *, layer, lam_init):
    k_tiles = _key_tiles(k_ref, (0,), k_ref.shape[1]) + _key_tiles(kc_ref, (0, 0), kc_ref.shape[2])
    v_tiles = _key_tiles(vo_ref, (0, 0), vo_ref.shape[2]) + _key_tiles(voc_ref, (0, 0, 0), voc_ref.shape[3])
    safe = _kept_safe_flag(k_tiles, qb2_ref[layer], safe_ref)
    group = _diff_group(q_ref, o_ref, 0, k_tiles, v_tiles, _diff_lambda(lam_ref, lam_init),
                        sg_ref[...] * (1.0 - lam_init))
    _attention(q_ref.shape[1], [group], [safe], r_ref, lhs_ref)


def _diff_call(qd, kd, vod, cache, layer, lam_p, subg, qb2, lam_init, tq):
    b, t, _ = qd.shape
    kdc, vodc = cache
    s = kdc.shape[2]
    return pl.pallas_call(
        functools.partial(_diff_kernel, layer=layer, lam_init=lam_init),
        out_shape=jax.ShapeDtypeStruct((b, t, DIFF_WIDTH), BF16),
        grid=(b, 2, t // tq),
        in_specs=[
            pl.BlockSpec((1, tq, LANES), lambda i, h, j: (i, j, h)),
            pl.BlockSpec((1, t, LANES), lambda i, h, j: (i, 0, h)),
            pl.BlockSpec((1, 1, t, 2 * LANES), lambda i, h, j: (i, h, 0, 0)),
            pl.BlockSpec((1, 1, s, LANES), lambda i, h, j: (i, layer, 0, h)),
            pl.BlockSpec((1, 1, 1, s, 2 * LANES), lambda i, h, j: (i, layer, h, 0, 0)),
            _layer_spec((4, DIFF_QK_DIM), layer), _layer_spec((1, LANES), layer), _SMEM_SPEC,
        ],
        out_specs=pl.BlockSpec((1, tq, LANES), lambda i, h, j: (i, j, h)),
        scratch_shapes=_attn_scratch(tq) + [pltpu.SMEM((1,), jnp.int32)],
        compiler_params=_latent_cparams(),
        name="diff_attention",
    )(qd, kd, vod, kdc, vodc, lam_p, subg, qb2)


def _ctx_attn_kernel(qg_ref, kk_ref, vog_ref, qd_ref, kd_ref, vod_ref, lam_ref, sg_ref, qb2g_ref, qb2d_ref,
                     og_ref, od_ref, r_ref, lhs_ref, *, layer, lam_init):
    t = qg_ref.shape[1]
    lam = _diff_lambda(lam_ref, lam_init)
    gain = sg_ref[...] * (1.0 - lam_init)
    groups, safes = [], []
    for kvh in range(GQA_KV_HEADS):
        k_tiles, v_tiles = _key_tiles(kk_ref, (0, kvh), t), _key_tiles(vog_ref, (0, kvh), t)
        groups.append(_gqa_group(qg_ref, og_ref, 2 * LANES * kvh, k_tiles, v_tiles))
        safes.append(_scores_are_safe(k_tiles, qb2g_ref[layer]))
    for pair in range(DIFF_HEADS // 2):
        lanes = slice(LANES * pair, LANES * (pair + 1))
        k_tiles, v_tiles = _key_tiles(kd_ref, (0,), t, lanes), _key_tiles(vod_ref, (0, pair), t)
        groups.append(_diff_group(qd_ref, od_ref, LANES * pair, k_tiles, v_tiles, lam, gain))
        safes.append(_scores_are_safe(k_tiles, qb2d_ref[layer]))
    _attention(t, groups, safes, r_ref, lhs_ref)


def _ctx_attn_call(qg, kk, vog, qd, kd, vod, layer, lam_p, subg, qb2_g, qb2_d, lam_init):
    b, t, _ = qg.shape
    tok = lambda w: pl.BlockSpec((1, t, w), lambda i: (i, 0, 0))
    pair = lambda w: pl.BlockSpec((1, 2, t, w), lambda i: (i, 0, 0, 0))
    return pl.pallas_call(
        functools.partial(_ctx_attn_kernel, layer=layer, lam_init=lam_init),
        out_shape=(jax.ShapeDtypeStruct((b, t, GQA_WIDTH), BF16), jax.ShapeDtypeStruct((b, t, DIFF_WIDTH), BF16)),
        grid=(b,),
        in_specs=[tok(GQA_WIDTH), pair(LANES), pair(2 * LANES), tok(DIFF_WIDTH), tok(DIFF_WIDTH), pair(2 * LANES),
                  _layer_spec((4, DIFF_QK_DIM), layer), _layer_spec((1, LANES), layer), _SMEM_SPEC, _SMEM_SPEC],
        out_specs=(tok(GQA_WIDTH), tok(DIFF_WIDTH)),
        scratch_shapes=_attn_scratch(t),
        compiler_params=_cparams(),
        name="context_attention",
    )(qg, kk, vog, qd, kd, vod, lam_p, subg, qb2_g, qb2_d)


def _post_kernel(*refs, halo, tm, t_len):
    it = iter(refs)
    n_blk = 3 if halo else 1

    def take():
        blocks = [next(it) for _ in range(n_blk)]
        if halo:
            return jnp.concatenate([blocks[0][0], blocks[1][0], blocks[2][0]], axis=0)
        return blocks[0][0]

    xe, og, od, cb, ccu = take(), take(), take(), take(), take()
    mod_ref, cw_ref, cbias_ref, wout_ref, n2g_ref = (next(it) for _ in range(5))
    up_ref, fcw_ref, fcb_ref, down_ref = (next(it) for _ in range(4))
    out_ref, ccu_scr, a_scr = next(it), next(it), next(it)

    h0 = HALO if halo else 0
    rows = tm + 2 * h0
    mods = mod_ref[0]
    g1, sh2, sc2, g2 = mods[2:3], mods[3:4], mods[4:5], mods[5:6]

    if halo:
        pos = pl.program_id(1) * tm - h0 + lax.broadcasted_iota(jnp.int32, (rows, 1), 0)
        valid = (pos >= 0) & (pos < t_len)
        ccu = jnp.where(valid, ccu, 0.0)

    ccu_scr[0:PAD] = jnp.zeros((PAD, CONV_WIDTH), F32)
    ccu_scr[PAD + rows: 2 * PAD + rows] = jnp.zeros((PAD, CONV_WIDTH), F32)
    ccu_scr[PAD: PAD + rows] = ccu
    cw = cw_ref[...]
    conv = (ccu_scr[PAD - 1: PAD - 1 + rows] * cw[0:1] + ccu * cw[1:2]
            + ccu_scr[PAD + 1: PAD + 1 + rows] * cw[2:3] + cbias_ref[...])
    oc = (cb * conv).astype(BF16)

    y = (jnp.dot(og, wout_ref[0:GQA_WIDTH], preferred_element_type=F32)
         + jnp.dot(oc, wout_ref[GQA_WIDTH: GQA_WIDTH + CONV_WIDTH], preferred_element_type=F32)
         + jnp.dot(od, wout_ref[GQA_WIDTH + CONV_WIDTH:], preferred_element_type=F32))
    xmid = xe + g1 * y

    inv = lax.rsqrt(jnp.mean(xmid * xmid, axis=-1, keepdims=True) + NORM_EPS)
    h2 = ((xmid * inv) * n2g_ref[...] * (1.0 + sc2) + sh2).astype(BF16)
    a = jnp.dot(h2, up_ref[:, :D_FF], preferred_element_type=F32)
    if halo:
        a = jnp.where(valid, a, 0.0)
    a_scr[0:PAD] = jnp.zeros((PAD, D_FF), F32)
    a_scr[PAD + rows: 2 * PAD + rows] = jnp.zeros((PAD, D_FF), F32)
    a_scr[PAD: PAD + rows] = a
    base = PAD + h0
    fcw = fcw_ref[...]
    ac = (a_scr[base - 1: base - 1 + tm] * fcw[0:1] + a_scr[base: base + tm] * fcw[1:2]
          + a_scr[base + 1: base + 1 + tm] * fcw[2:3] + fcb_ref[...])
    u = jnp.dot(h2[h0: h0 + tm], up_ref[:, D_FF:], preferred_element_type=F32)
    f = (ac * jax.nn.sigmoid(ac) * u).astype(BF16)
    out_ref[0] = xmid[h0: h0 + tm] + g2 * jnp.dot(f, down_ref[...], preferred_element_type=F32)


def _post_call(x, og, od, cb, ccu, layer, mods, mod_row, cw, cbias, w_out, n2g, up, fcw, fcb, down, tm):
    b, t, d = x.shape
    nt = t // tm
    halo = nt > 1
    per = tm // HALO
    last = t // HALO - 1

    def specs(width):
        main = pl.BlockSpec((1, tm, width), lambda i, j: (i, j, 0))
        if not halo:
            return [main]
        prev = pl.BlockSpec((1, HALO, width), lambda i, j: (i, jnp.maximum(j * per - 1, 0), 0))
        nxt = pl.BlockSpec((1, HALO, width), lambda i, j: (i, jnp.minimum((j + 1) * per, last), 0))
        return [prev, main, nxt]

    in_specs, args = [], []
    for arr in (x, og, od, cb, ccu):
        sp = specs(arr.shape[-1])
        in_specs += sp
        args += [arr] * len(sp)
    in_specs += [
        _mods_spec(layer, mod_row),
        _layer_spec((3, CONV_WIDTH), layer), _layer_spec((1, CONV_WIDTH), layer),
        _layer_spec((d, d), layer), _layer_spec((1, d), layer),
        _layer_spec((d, 2 * D_FF), layer), _layer_spec((3, D_FF), layer), _layer_spec((1, D_FF), layer),
        _layer_spec((D_FF, d), layer),
    ]
    args += [mods, cw, cbias, w_out, n2g, up, fcw, fcb, down]
    rows = tm + (2 * HALO if halo else 0)
    return pl.pallas_call(
        functools.partial(_post_kernel, halo=halo, tm=tm, t_len=t),
        out_shape=jax.ShapeDtypeStruct((b, t, d), F32),
        grid=(b, nt),
        in_specs=in_specs,
        out_specs=pl.BlockSpec((1, tm, d), lambda i, j: (i, j, 0)),
        scratch_shapes=[pltpu.VMEM((rows + 2 * PAD, CONV_WIDTH), F32),
                        pltpu.VMEM((rows + 2 * PAD, D_FF), F32)],
        compiler_params=_cparams(),
        name="post_attention",
    )(*args)


def _rope_tables(n_rows, dim):
    row = jnp.repeat(jnp.arange(n_rows), GRID_W).astype(F32)
    col = jnp.tile(jnp.arange(GRID_W), n_rows).astype(F32)
    nf = dim // 4
    freqs = ROPE_THETA ** (-jnp.arange(nf, dtype=F32) / nf)
    ar = row[:, None] * freqs[None, :]
    ac = col[:, None] * freqs[None, :]
    ang = jnp.concatenate([ar, ar, ac, ac], axis=-1)
    cos, sin = jnp.cos(ang), jnp.sin(ang)
    first = ((jnp.arange(dim) // nf) % 2 == 0)[None, :]
    sin_up = jnp.where(first, -sin, 0.0)
    sin_dn = jnp.where(first, 0.0, sin)
    rep = LANES // dim
    return tuple(jnp.tile(tab, (1, rep)) for tab in (cos, sin_up, sin_dn))


def _group_matrix(group):
    idx = jnp.arange(256) // group
    return (idx[:, None] == idx[None, :]).astype(BF16)


def kernel(x_prompt, x_sample, cache_gqa_k, cache_gqa_v, cache_diff_k, cache_diff_v, c, c_ctx,
           w_mod, b_mod, norm1_g, w_in, gqa_qn_g, gqa_kn_g, conv_w, conv_b, diff_qn_g, diff_kn_g,
           diff_lambda, diff_subln_g, w_out, norm2_g, ffn_up, ffn_conv_w, ffn_conv_b, ffn_down):
    depth = w_in.shape[0]
    batch, seq, _ = x_prompt.shape
    dec_batch, dec_seq, _ = x_sample.shape

    cond_all = jnp.zeros((MOD_ROWS, D_MODEL), F32).at[0].set(c_ctx).at[1:1 + dec_batch].set(c)
    mods = _mods_call(cond_all, w_mod, b_mod)
    kkc, vogc, kdc, vodc = _cache_prep_call(cache_gqa_k, cache_gqa_v, cache_diff_k, cache_diff_v)

    w_in_b = w_in.astype(BF16)
    w_out_b = w_out.astype(BF16)
    up_b = ffn_up.astype(BF16)
    down_b = ffn_down.astype(BF16)
    g64, g32 = _group_matrix(HEAD_DIM), _group_matrix(DIFF_QK_DIM)
    rope_tabs = _rope_tables(dec_seq // GRID_W, HEAD_DIM) + _rope_tables(dec_seq // GRID_W, DIFF_QK_DIM)

    row = lambda p: p.reshape(depth, 1, -1)
    lane_tiled = lambda p, rep: jnp.tile(p, (1, rep)).reshape(depth, 1, LANES)
    gq = lane_tiled(gqa_qn_g * (HEAD_DIM ** -0.5 * LOG2E), 2)
    gk = lane_tiled(gqa_kn_g, 2)
    gqd = lane_tiled(diff_qn_g * (DIFF_QK_DIM ** -0.5 * LOG2E), 4)
    gkd = lane_tiled(diff_kn_g, 4)
    subg = lane_tiled(diff_subln_g, 2)
    qb2_g = _query_norm_bound(gq, HEAD_DIM)
    qb2_d = _query_norm_bound(gqd, DIFF_QK_DIM)
    n1g, n2g, cbias, fcb = row(norm1_g), row(norm2_g), row(conv_b), row(ffn_conv_b)

    def layer(x, l, mod_row, rope, cache_g, cache_d, own, tp, tm, tq):
        lam_init = 0.8 - 0.6 * math.exp(-0.3 * l)
        pre = _pre_call(x, l, mods, mod_row, n1g, w_in_b, gq, gk, gqd, gkd, g64, g32, rope, own, tp)
        qg, kk, vog, qd, kd, vod, cb, ccu = pre[:8]
        if cache_g is None:
            og, od = _ctx_attn_call(qg, kk, vog, qd, kd, vod, l, diff_lambda, subg, qb2_g, qb2_d, lam_init)
        else:
            og = _gqa_call(qg, kk, vog, cache_g, l, qb2_g, tq)
            od = _diff_call(qd, kd, vod, cache_d, l, diff_lambda, subg, qb2_d, lam_init, tq)
        x = _post_call(x, og, od, cb, ccu, l, mods, mod_row, conv_w, cbias, w_out_b, n2g, up_b,
                       ffn_conv_w, fcb, down_b, tm)
        return x, pre[8:]

    xp = x_prompt
    owns = []
    for l in range(depth):
        xp, own = layer(xp, l, lambda i: 0, None, None, None, True, seq, seq, seq)
        owns.append(own)
    new_gqa_k = jnp.stack([o[0] for o in owns], axis=1).reshape(batch, depth, seq, GQA_KV_HEADS, HEAD_DIM)
    new_gqa_v = jnp.stack([o[1] for o in owns], axis=1).reshape(batch, depth, seq, GQA_KV_HEADS, HEAD_DIM)
    new_diff_k = jnp.stack([o[2] for o in owns], axis=1).reshape(batch, depth, seq, DIFF_HEADS, 2, DIFF_QK_DIM)
    new_diff_v = jnp.stack([o[3] for o in owns], axis=1).reshape(batch, depth, seq, DIFF_HEADS, DIFF_V_DIM)

    xs = x_sample
    for l in range(depth):
        xs, _ = layer(xs, l, lambda i: i + 1, rope_tabs, (kkc, vogc), (kdc, vodc), False, 1024, 512, 1024)

    return (xp, xs, new_gqa_k, new_gqa_v, new_diff_k, new_diff_v)
```

```python
import functools
import math

import jax
import jax.numpy as jnp
from jax import lax
from jax.experimental import pallas as pl
from jax.experimental.pallas import tpu as pltpu

F32 = jnp.float32
BF16 = jnp.bfloat16

D_MODEL = 1024
GRID_W = 64
HEAD_DIM = 64
GQA_HEADS = 8
GQA_KV_HEADS = 2
GQA_WIDTH = GQA_HEADS * HEAD_DIM
CONV_WIDTH = 256
DIFF_HEADS = 4
DIFF_QK_DIM = 32
DIFF_V_DIM = 64
DIFF_WIDTH = DIFF_HEADS * DIFF_V_DIM
IN_WIDTH = 2304
D_FF = 2816
ROPE_THETA = 10000.0
NORM_EPS = 1e-6
N_MOD = 6
LANES = 128
MOD_ROWS = 16

OFF_QG, OFF_KG, OFF_VG = 0, 512, 640
OFF_CB, OFF_CC, OFF_CU = 768, 1024, 1280
OFF_QD, OFF_KD, OFF_VD = 1536, 1792, 2048

HALO = 16
PAD = 8

VMEM_LIMIT = 60 * 1024 * 1024

KEY_TILE = 256
SAFE_EXPONENT = 60.0
LOG2E = math.log2(math.e)
PRE_ROWS = 128
QUERY_BLOCK = 64
EXACT_ROWS = 512


def _cparams():
    return pltpu.CompilerParams(vmem_limit_bytes=VMEM_LIMIT)


def _const_spec(shape):
    nd = len(shape)
    return pl.BlockSpec(shape, lambda *_: (0,) * nd, pipeline_mode=pl.Buffered(1))


def _layer_spec(shape, layer):
    nd = len(shape)
    return pl.BlockSpec((None,) + tuple(shape), lambda *_: (layer,) + (0,) * nd, pipeline_mode=pl.Buffered(1))


def _mods_spec(layer, mod_row):
    return pl.BlockSpec((None, 1, N_MOD, D_MODEL), lambda i, j: (layer, mod_row(i), 0, 0))


def _mods_kernel(cond_ref, w_ref, b_ref, o_ref):
    cnd = cond_ref[...]
    act = cnd * jax.nn.sigmoid(cnd)
    o_ref[0] = jnp.dot(act, w_ref[0], preferred_element_type=F32,
                       precision=lax.Precision.HIGHEST) + b_ref[0]


def _mods_call(cond_all, w_mod, b_mod):
    depth = w_mod.shape[0]
    width = N_MOD * D_MODEL
    tn = 1024
    out = pl.pallas_call(
        _mods_kernel,
        out_shape=jax.ShapeDtypeStruct((depth, MOD_ROWS, width), F32),
        grid=(depth, width // tn),
        in_specs=[
            pl.BlockSpec((MOD_ROWS, D_MODEL), lambda l, j: (0, 0)),
            pl.BlockSpec((1, D_MODEL, tn), lambda l, j: (l, 0, j)),
            pl.BlockSpec((1, 1, tn), lambda l, j: (l, 0, j)),
        ],
        out_specs=pl.BlockSpec((1, MOD_ROWS, tn), lambda l, j: (l, 0, j)),
        compiler_params=_cparams(),
        name="adaln_mods",
    )(cond_all, w_mod, b_mod.reshape(depth, 1, width))
    return out.reshape(depth, MOD_ROWS, N_MOD, D_MODEL)


def _dup_halves(x128):
    lo = lax.broadcasted_iota(jnp.int32, x128.shape, 1) < HEAD_DIM
    swapped = pltpu.roll(x128, HEAD_DIM, 1)
    return jnp.where(lo, x128, swapped), jnp.where(lo, swapped, x128)


def _cache_prep_kernel(gk_ref, gv_ref, dk_ref, dv_ref, kk_ref, vog_ref, kd_ref, vod_ref):
    ones = jnp.ones((gk_ref.shape[2], LANES), BF16)
    k0, k1 = _dup_halves(gk_ref[0, 0])
    kk_ref[0, 0, 0] = k0.astype(BF16)
    kk_ref[0, 0, 1] = k1.astype(BF16)
    v0, v1 = _dup_halves(gv_ref[0, 0])
    vog_ref[0, 0, 0, :, :LANES] = v0.astype(BF16)
    vog_ref[0, 0, 1, :, :LANES] = v1.astype(BF16)
    vog_ref[0, 0, 0, :, LANES:] = ones
    vog_ref[0, 0, 1, :, LANES:] = ones
    kd_ref[0, 0] = dk_ref[0, 0].astype(BF16)
    dv = dv_ref[0, 0]
    vod_ref[0, 0, 0, :, :LANES] = dv[:, :LANES].astype(BF16)
    vod_ref[0, 0, 1, :, :LANES] = dv[:, LANES:].astype(BF16)
    vod_ref[0, 0, 0, :, LANES:] = ones
    vod_ref[0, 0, 1, :, LANES:] = ones


def _cache_prep_call(cache_gqa_k, cache_gqa_v, cache_diff_k, cache_diff_v):
    b, depth, s = cache_gqa_k.shape[:3]
    gk = cache_gqa_k.reshape(b, depth, s, LANES)
    gv = cache_gqa_v.reshape(b, depth, s, LANES)
    dk = cache_diff_k.reshape(b, depth, s, 2 * LANES)
    dv = cache_diff_v.reshape(b, depth, s, 2 * LANES)
    idx4 = lambda i, l: (i, l, 0, 0)
    idx5 = lambda i, l: (i, l, 0, 0, 0)
    return pl.pallas_call(
        _cache_prep_kernel,
        out_shape=(
            jax.ShapeDtypeStruct((b, depth, 2, s, LANES), BF16),
            jax.ShapeDtypeStruct((b, depth, 2, s, 2 * LANES), BF16),
            jax.ShapeDtypeStruct((b, depth, s, 2 * LANES), BF16),
            jax.ShapeDtypeStruct((b, depth, 2, s, 2 * LANES), BF16),
        ),
        grid=(b, depth),
        in_specs=[
            pl.BlockSpec((1, 1, s, LANES), idx4),
            pl.BlockSpec((1, 1, s, LANES), idx4),
            pl.BlockSpec((1, 1, s, 2 * LANES), idx4),
            pl.BlockSpec((1, 1, s, 2 * LANES), idx4),
        ],
        out_specs=(
            pl.BlockSpec((1, 1, 2, s, LANES), idx5),
            pl.BlockSpec((1, 1, 2, s, 2 * LANES), idx5),
            pl.BlockSpec((1, 1, s, 2 * LANES), idx4),
            pl.BlockSpec((1, 1, 2, s, 2 * LANES), idx5),
        ),
        compiler_params=_cparams(),
        name="cache_prep",
    )(gk, gv, dk, dv)


def _group_inv_rms(z, g_ref, group):
    sq = z * z
    hi = sq.astype(BF16)
    lo = (sq - hi.astype(F32)).astype(BF16)
    width = z.shape[1]
    gmat = g_ref[:width, :width]
    ssum = (jnp.dot(hi, gmat, preferred_element_type=F32)
            + jnp.dot(lo, gmat, preferred_element_type=F32))
    return lax.rsqrt(ssum * (1.0 / group) + NORM_EPS)


def _rope_chunk(y, cos, sin_up, sin_dn, quarter):
    up = pltpu.roll(y, LANES - quarter, 1)
    dn = pltpu.roll(y, quarter, 1)
    return y * cos + up * sin_up + dn * sin_dn


def _pre_kernel(*refs, rope, own):
    it = iter(refs)
    x_ref, mod_ref, n1g_ref, w_in_ref = next(it), next(it), next(it), next(it)
    gq_ref, gk_ref, gqd_ref, gkd_ref = next(it), next(it), next(it), next(it)
    g64_ref, g32_ref = next(it), next(it)
    if rope:
        c64, su64, sd64, c32, su32, sd32 = (next(it) for _ in range(6))
    qg_ref, kk_ref, vog_ref, qd_ref, kd_ref, vod_ref, cb_ref, ccu_ref = (next(it) for _ in range(8))
    if own:
        kown_ref, vown_ref, kdown_ref, vdown_ref = (next(it) for _ in range(4))

    mods = mod_ref[0]
    sh1, sc1 = mods[0:1], mods[1:2]
    gain1 = n1g_ref[...] * (1.0 + sc1)
    ones = jnp.ones((PRE_ROWS, LANES), BF16)

    for r in range(x_ref.shape[1] // PRE_ROWS):
        rs = slice(r * PRE_ROWS, (r + 1) * PRE_ROWS)
        x = x_ref[0, rs]
        inv = lax.rsqrt(jnp.mean(x * x, axis=-1, keepdims=True) + NORM_EPS)
        h = (x * inv) * gain1 + sh1
        z = jnp.dot(h.astype(BF16), w_in_ref[...], preferred_element_type=F32)

        def rope64(y):
            if not rope:
                return y
            return _rope_chunk(y, c64[rs, :], su64[rs, :], sd64[rs, :], HEAD_DIM // 4)

        def rope32(y):
            if not rope:
                return y
            return _rope_chunk(y, c32[rs, :], su32[rs, :], sd32[rs, :], DIFF_QK_DIM // 4)

        for c in range(2):
            zc = z[:, OFF_QG + 256 * c: OFF_QG + 256 * (c + 1)]
            y = zc * _group_inv_rms(zc, g64_ref, HEAD_DIM)
            for j in range(2):
                yj = rope64(y[:, LANES * j: LANES * (j + 1)] * gq_ref[...])
                qg_ref[0, rs, 256 * c + LANES * j: 256 * c + LANES * (j + 1)] = yj.astype(BF16)

        zk = z[:, OFF_KG: OFF_KG + LANES]
        zv = z[:, OFF_VG: OFF_VG + LANES]
        kn = zk * _group_inv_rms(zk, g64_ref, HEAD_DIM) * gk_ref[...]
        if own:
            kown_ref[0, rs] = kn
            vown_ref[0, rs] = zv
        k0, k1 = _dup_halves(rope64(kn))
        kk_ref[0, 0, rs] = k0.astype(BF16)
        kk_ref[0, 1, rs] = k1.astype(BF16)
        v0, v1 = _dup_halves(zv)
        vog_ref[0, 0, rs, :LANES] = v0.astype(BF16)
        vog_ref[0, 1, rs, :LANES] = v1.astype(BF16)
        vog_ref[0, 0, rs, LANES:] = ones
        vog_ref[0, 1, rs, LANES:] = ones

        cb_ref[0, rs] = z[:, OFF_CB: OFF_CB + CONV_WIDTH]
        ccu_ref[0, rs] = z[:, OFF_CC: OFF_CC + CONV_WIDTH] * z[:, OFF_CU: OFF_CU + CONV_WIDTH]

        zq = z[:, OFF_QD: OFF_QD + 256]
        yq = zq * _group_inv_rms(zq, g32_ref, DIFF_QK_DIM)
        zkd = z[:, OFF_KD: OFF_KD + 256]
        ykd = zkd * _group_inv_rms(zkd, g32_ref, DIFF_QK_DIM)
        zvd = z[:, OFF_VD: OFF_VD + 256]
        for j in range(2):
            sl = slice(LANES * j, LANES * (j + 1))
            qd_ref[0, rs, sl] = rope32(yq[:, sl] * gqd_ref[...]).astype(BF16)
            kdj = ykd[:, sl] * gkd_ref[...]
            if own:
                kdown_ref[0, rs, sl] = kdj
            kd_ref[0, rs, sl] = rope32(kdj).astype(BF16)
            vod_ref[0, j, rs, :LANES] = zvd[:, sl].astype(BF16)
            vod_ref[0, j, rs, LANES:] = ones
        if own:
            vdown_ref[0, rs] = zvd


def _pre_call(x, layer, mods, mod_row, n1g, w_in, gq, gk, gqd, gkd, g64, g32, rope_tabs, own, tm):
    b, t, d = x.shape
    nt = t // tm
    rope = rope_tabs is not None
    gain_spec = _layer_spec((1, LANES), layer)
    in_specs = [
        pl.BlockSpec((1, tm, d), lambda i, j: (i, j, 0)),
        _mods_spec(layer, mod_row),
        _layer_spec((1, d), layer),
        _layer_spec((d, IN_WIDTH), layer),
        gain_spec, gain_spec, gain_spec, gain_spec,
        _const_spec((256, 256)), _const_spec((256, 256)),
    ]
    args = [x, mods, n1g, w_in, gq, gk, gqd, gkd, g64, g32]
    if rope:
        in_specs += [pl.BlockSpec((tm, LANES), lambda i, j: (j, 0))] * 6
        args += list(rope_tabs)
    tok = lambda w, dt: jax.ShapeDtypeStruct((b, t, w), dt)
    tok_spec = lambda w: pl.BlockSpec((1, tm, w), lambda i, j: (i, j, 0))
    pair = lambda w: jax.ShapeDtypeStruct((b, 2, t, w), BF16)
    pair_spec = lambda w: pl.BlockSpec((1, 2, tm, w), lambda i, j: (i, 0, j, 0))
    out_shape = [tok(GQA_WIDTH, BF16), pair(LANES), pair(2 * LANES), tok(256, BF16), tok(256, BF16),
                 pair(2 * LANES), tok(CONV_WIDTH, F32), tok(CONV_WIDTH, F32)]
    out_specs = [tok_spec(GQA_WIDTH), pair_spec(LANES), pair_spec(2 * LANES), tok_spec(256), tok_spec(256),
                 pair_spec(2 * LANES), tok_spec(CONV_WIDTH), tok_spec(CONV_WIDTH)]
    if own:
        out_shape += [tok(LANES, F32), tok(LANES, F32), tok(256, F32), tok(256, F32)]
        out_specs += [tok_spec(LANES), tok_spec(LANES), tok_spec(256), tok_spec(256)]
    return pl.pallas_call(
        functools.partial(_pre_kernel, rope=rope, own=own),
        out_shape=tuple(out_shape),
        grid=(b, nt),
        in_specs=in_specs,
        out_specs=tuple(out_specs),
        compiler_params=_cparams(),
        name="pre_attention",
    )(*args)


def _key_tiles(ref, lead, n_rows, lanes=slice(None)):
    def load(t):
        return ref[lead + (pl.ds(t * KEY_TILE, KEY_TILE), lanes)]
    return [functools.partial(load, t) for t in range(n_rows // KEY_TILE)]


def _row_norm2_max(x):
    xf = x.astype(F32)
    return jnp.max(jnp.sum(xf * xf, axis=-1, keepdims=True))


def _dot_nt(a, b):
    return lax.dot_general(a, b, (((1,), (1,)), ((), ())), preferred_element_type=F32)


def _scores_are_safe(k_tiles, qb2):
    kmax2 = functools.reduce(jnp.maximum, [_row_norm2_max(k()) for k in k_tiles])
    return qb2 * kmax2 <= SAFE_EXPONENT ** 2


def _softmax_pv_shifted(lhs_ref, k_tiles, v_tiles, r_ref):
    def row_block(rb, carry):
        rows = pl.ds(pl.multiple_of(rb * EXACT_ROWS, EXACT_ROWS), EXACT_ROWS)
        lhs_b = lhs_ref[rows, :]
        scores = [_dot_nt(lhs_b, k()) for k in k_tiles]
        m = functools.reduce(jnp.maximum, [jnp.max(s, axis=-1, keepdims=True) for s in scores])
        acc = None
        for s, v in zip(scores, v_tiles):
            p = jnp.exp2(s - m).astype(BF16)
            part = jnp.dot(p, v(), preferred_element_type=F32)
            acc = part if acc is None else acc + part
        r_ref[rows, :] = acc[:, :LANES] / acc[:, LANES:]
        return carry

    lax.fori_loop(0, lhs_ref.shape[0] // EXACT_ROWS, row_block, 0)


def _group_unshifted(tq, group):
    lhs_block, finish, k_tiles, v_tiles = group
    for qb in range(tq // QUERY_BLOCK):
        lhs_b = lhs_block(qb)
        acc = None
        for k, v in zip(k_tiles, v_tiles):
            p = jnp.exp2(_dot_nt(lhs_b, k())).astype(BF16)
            part = jnp.dot(p, v(), preferred_element_type=F32)
            acc = part if acc is None else acc + part
        finish(acc[:, :LANES] / acc[:, LANES:], qb)


def _group_shifted(tq, group, r_ref, lhs_ref):
    lhs_block, finish, k_tiles, v_tiles = group
    rows = 4 * QUERY_BLOCK
    for qb in range(tq // QUERY_BLOCK):
        lhs_ref[qb * rows:(qb + 1) * rows, :] = lhs_block(qb)
    _softmax_pv_shifted(lhs_ref, k_tiles, v_tiles, r_ref)
    for qb in range(tq // QUERY_BLOCK):
        finish(r_ref[qb * rows:(qb + 1) * rows, :], qb)


def _attention(tq, groups, safes, r_ref, lhs_ref):
    all_safe = functools.reduce(jnp.logical_and, safes)

    @pl.when(all_safe)
    def _():
        for group in groups:
            _group_unshifted(tq, group)

    @pl.when(jnp.logical_not(all_safe))
    def _():
        for group, safe in zip(groups, safes):
            if len(groups) > 1:
                pl.when(safe)(functools.partial(_group_unshifted, tq, group))
            pl.when(jnp.logical_not(safe))(functools.partial(_group_shifted, tq, group, r_ref, lhs_ref))


def _gqa_group(q_ref, o_ref, lane0, k_tiles, v_tiles):
    nq = QUERY_BLOCK
    lo = lax.broadcasted_iota(jnp.int32, (nq, LANES), 1) < HEAD_DIM
    zero = jnp.zeros((nq, LANES), BF16)

    def lhs_block(qb):
        rows = slice(qb * nq, (qb + 1) * nq)
        qa = q_ref[0, rows, lane0:lane0 + LANES]
        qc = q_ref[0, rows, lane0 + LANES:lane0 + 2 * LANES]
        return jnp.concatenate([jnp.where(lo, qa, zero), jnp.where(lo, zero, qa),
                                jnp.where(lo, qc, zero), jnp.where(lo, zero, qc)], axis=0)

    def finish(r, qb):
        rows = slice(qb * nq, (qb + 1) * nq)
        o_ref[0, rows, lane0:lane0 + LANES] = jnp.where(lo, r[0:nq], r[nq:2 * nq]).astype(BF16)
        o_ref[0, rows, lane0 + LANES:lane0 + 2 * LANES] = (
            jnp.where(lo, r[2 * nq:3 * nq], r[3 * nq:4 * nq]).astype(BF16))

    return lhs_block, finish, k_tiles, v_tiles


def _diff_lambda(lam_ref, lam_init):
    lf = lam_ref[...]
    return (jnp.exp(jnp.sum(lf[0:1] * lf[1:2], axis=-1, keepdims=True))
            - jnp.exp(jnp.sum(lf[2:3] * lf[3:4], axis=-1, keepdims=True)) + lam_init)


def _diff_group(q_ref, o_ref, lane0, k_tiles, v_tiles, lam, gain):
    nq = QUERY_BLOCK
    lane = lax.broadcasted_iota(jnp.int32, (nq, LANES), 1)
    lo = lane < DIFF_V_DIM
    zero = jnp.zeros((nq, LANES), BF16)

    def lhs_block(qb):
        q = q_ref[0, qb * nq:(qb + 1) * nq, lane0:lane0 + LANES]
        return jnp.concatenate(
            [jnp.where((lane >= DIFF_QK_DIM * u) & (lane < DIFF_QK_DIM * (u + 1)), q, zero) for u in range(4)],
            axis=0)

    def finish(r, qb):
        o = jnp.where(lo, r[0:nq] - lam * r[nq:2 * nq], r[2 * nq:3 * nq] - lam * r[3 * nq:4 * nq])
        sq = o * o
        ms_lo = jnp.sum(jnp.where(lo, sq, 0.0), axis=-1, keepdims=True) * (1.0 / DIFF_V_DIM)
        ms_hi = jnp.sum(jnp.where(lo, 0.0, sq), axis=-1, keepdims=True) * (1.0 / DIFF_V_DIM)
        inv = jnp.where(lo, lax.rsqrt(ms_lo + NORM_EPS), lax.rsqrt(ms_hi + NORM_EPS))
        o_ref[0, qb * nq:(qb + 1) * nq, lane0:lane0 + LANES] = ((o * inv) * gain).astype(BF16)

    return lhs_block, finish, k_tiles, v_tiles


def _query_norm_bound(gain, dim):
    return dim * 1.02 * jnp.max(gain * gain, axis=(1, 2))


_SMEM_SPEC = pl.BlockSpec(memory_space=pltpu.SMEM)


def _attn_scratch(tq):
    return [pltpu.VMEM((4 * tq, LANES), F32), pltpu.VMEM((4 * tq, LANES), BF16)]


def _kept_safe_flag(k_tiles, qb2, safe_ref):
    @pl.when(pl.program_id(2) == 0)
    def _():
        safe_ref[0] = _scores_are_safe(k_tiles, qb2).astype(jnp.int32)
    return safe_ref[0] == 1


def _latent_cparams():
    return pltpu.CompilerParams(vmem_limit_bytes=VMEM_LIMIT,
                                dimension_semantics=("arbitrary", "arbitrary", "arbitrary"))


def _gqa_kernel(q_ref, kk_ref, vo_ref, kkc_ref, voc_ref, qb2_ref, o_ref, r_ref, lhs_ref, safe_ref, *, layer):
    k_tiles = _key_tiles(kk_ref, (0, 0), kk_ref.shape[2]) + _key_tiles(kkc_ref, (0, 0, 0), kkc_ref.shape[3])
    v_tiles = _key_tiles(vo_ref, (0, 0), vo_ref.shape[2]) + _key_tiles(voc_ref, (0, 0, 0), voc_ref.shape[3])
    safe = _kept_safe_flag(k_tiles, qb2_ref[layer], safe_ref)
    _attention(q_ref.shape[1], [_gqa_group(q_ref, o_ref, 0, k_tiles, v_tiles)], [safe], r_ref, lhs_ref)


def _gqa_call(qg, kk, vog, cache, layer, qb2, tq):
    b, t, _ = qg.shape
    kkc, voc = cache
    s = kkc.shape[3]
    return pl.pallas_call(
        functools.partial(_gqa_kernel, layer=layer),
        out_shape=jax.ShapeDtypeStruct((b, t, GQA_WIDTH), BF16),
        grid=(b, GQA_KV_HEADS, t // tq),
        in_specs=[
            pl.BlockSpec((1, tq, 256), lambda i, h, j: (i, j, h)),
            pl.BlockSpec((1, 1, t, LANES), lambda i, h, j: (i, h, 0, 0)),
            pl.BlockSpec((1, 1, t, 2 * LANES), lambda i, h, j: (i, h, 0, 0)),
            pl.BlockSpec((1, 1, 1, s, LANES), lambda i, h, j: (i, layer, h, 0, 0)),
            pl.BlockSpec((1, 1, 1, s, 2 * LANES), lambda i, h, j: (i, layer, h, 0, 0)),
            _SMEM_SPEC,
        ],
        out_specs=pl.BlockSpec((1, tq, 256), lambda i, h, j: (i, j, h)),
        scratch_shapes=_attn_scratch(tq) + [pltpu.SMEM((1,), jnp.int32)],
        compiler_params=_latent_cparams(),
        name="gqa_attention",
    )(qg, kk, vog, kkc, voc, qb2)


def _diff_kernel(q_ref, k_ref, vo_ref, kc_ref, voc_ref, lam_ref, sg_ref, qb2_ref, o_ref, r_ref, lhs_ref, safe_ref,
                 *, layer, lam_init):
    k_tiles = _key_tiles(k_ref, (0,), k_ref.shape[1]) + _key_tiles(kc_ref, (0, 0), kc_ref.shape[2])
    v_tiles = _key_tiles(vo_ref, (0, 0), vo_ref.shape[2]) + _key_tiles(voc_ref, (0, 0, 0), voc_ref.shape[3])
    safe = _kept_safe_flag(k_tiles, qb2_ref[layer], safe_ref)
    group = _diff_group(q_ref, o_ref, 0, k_tiles, v_tiles, _diff_lambda(lam_ref, lam_init),
                        sg_ref[...] * (1.0 - lam_init))
    _attention(q_ref.shape[1], [group], [safe], r_ref, lhs_ref)


def _diff_call(qd, kd, vod, cache, layer, lam_p, subg, qb2, lam_init, tq):
    b, t, _ = qd.shape
    kdc, vodc = cache
    s = kdc.shape[2]
    return pl.pallas_call(
        functools.partial(_diff_kernel, layer=layer, lam_init=lam_init),
        out_shape=jax.ShapeDtypeStruct((b, t, DIFF_WIDTH), BF16),
        grid=(b, 2, t // tq),
        in_specs=[
            pl.BlockSpec((1, tq, LANES), lambda i, h, j: (i, j, h)),
            pl.BlockSpec((1, t, LANES), lambda i, h, j: (i, 0, h)),
            pl.BlockSpec((1, 1, t, 2 * LANES), lambda i, h, j: (i, h, 0, 0)),
            pl.BlockSpec((1, 1, s, LANES), lambda i, h, j: (i, layer, 0, h)),
            pl.BlockSpec((1, 1, 1, s, 2 * LANES), lambda i, h, j: (i, layer, h, 0, 0)),
            _layer_spec((4, DIFF_QK_DIM), layer), _layer_spec((1, LANES), layer), _SMEM_SPEC,
        ],
        out_specs=pl.BlockSpec((1, tq, LANES), lambda i, h, j: (i, j, h)),
        scratch_shapes=_attn_scratch(tq) + [pltpu.SMEM((1,), jnp.int32)],
        compiler_params=_latent_cparams(),
        name="diff_attention",
    )(qd, kd, vod, kdc, vodc, lam_p, subg, qb2)


def _ctx_attn_kernel(qg_ref, kk_ref, vog_ref, qd_ref, kd_ref, vod_ref, lam_ref, sg_ref, qb2g_ref, qb2d_ref,
                     og_ref, od_ref, r_ref, lhs_ref, *, layer, lam_init):
    t = qg_ref.shape[1]
    lam = _diff_lambda(lam_ref, lam_init)
    gain = sg_ref[...] * (1.0 - lam_init)
    groups, safes = [], []
    for kvh in range(GQA_KV_HEADS):
        k_tiles, v_tiles = _key_tiles(kk_ref, (0, kvh), t), _key_tiles(vog_ref, (0, kvh), t)
        groups.append(_gqa_group(qg_ref, og_ref, 2 * LANES * kvh, k_tiles, v_tiles))
        safes.append(_scores_are_safe(k_tiles, qb2g_ref[layer]))
    for pair in range(DIFF_HEADS // 2):
        lanes = slice(LANES * pair, LANES * (pair + 1))
        k_tiles, v_tiles = _key_tiles(kd_ref, (0,), t, lanes), _key_tiles(vod_ref, (0, pair), t)
        groups.append(_diff_group(qd_ref, od_ref, LANES * pair, k_tiles, v_tiles, lam, gain))
        safes.append(_scores_are_safe(k_tiles, qb2d_ref[layer]))
    _attention(t, groups, safes, r_ref, lhs_ref)


def _ctx_attn_call(qg, kk, vog, qd, kd, vod, layer, lam_p, subg, qb2_g, qb2_d, lam_init):
    b, t, _ = qg.shape
    tok = lambda w: pl.BlockSpec((1, t, w), lambda i: (i, 0, 0))
    pair = lambda w: pl.BlockSpec((1, 2, t, w), lambda i: (i, 0, 0, 0))
    return pl.pallas_call(
        functools.partial(_ctx_attn_kernel, layer=layer, lam_init=lam_init),
        out_shape=(jax.ShapeDtypeStruct((b, t, GQA_WIDTH), BF16), jax.ShapeDtypeStruct((b, t, DIFF_WIDTH), BF16)),
        grid=(b,),
        in_specs=[tok(GQA_WIDTH), pair(LANES), pair(2 * LANES), tok(DIFF_WIDTH), tok(DIFF_WIDTH), pair(2 * LANES),
                  _layer_spec((4, DIFF_QK_DIM), layer), _layer_spec((1, LANES), layer), _SMEM_SPEC, _SMEM_SPEC],
        out_specs=(tok(GQA_WIDTH), tok(DIFF_WIDTH)),
        scratch_shapes=_attn_scratch(t),
        compiler_params=_cparams(),
        name="context_attention",
    )(qg, kk, vog, qd, kd, vod, lam_p, subg, qb2_g, qb2_d)


def _post_kernel(*refs, halo, tm, t_len):
    it = iter(refs)
    n_blk = 3 if halo else 1

    def take():
        blocks = [next(it) for _ in range(n_blk)]
        if halo:
            return jnp.concatenate([blocks[0][0], blocks[1][0], blocks[2][0]], axis=0)
        return blocks[0][0]

    xe, og, od, cb, ccu = take(), take(), take(), take(), take()
    mod_ref, cw_ref, cbias_ref, wout_ref, n2g_ref = (next(it) for _ in range(5))
    up_ref, fcw_ref, fcb_ref, down_ref = (next(it) for _ in range(4))
    out_ref, ccu_scr, a_scr = next(it), next(it), next(it)

    h0 = HALO if halo else 0
    rows = tm + 2 * h0
    mods = mod_ref[0]
    g1, sh2, sc2, g2 = mods[2:3], mods[3:4], mods[4:5], mods[5:6]

    if halo:
        pos = pl.program_id(1) * tm - h0 + lax.broadcasted_iota(jnp.int32, (rows, 1), 0)
        valid = (pos >= 0) & (pos < t_len)
        ccu = jnp.where(valid, ccu, 0.0)

    ccu_scr[0:PAD] = jnp.zeros((PAD, CONV_WIDTH), F32)
    ccu_scr[PAD + rows: 2 * PAD + rows] = jnp.zeros((PAD, CONV_WIDTH), F32)
    ccu_scr[PAD: PAD + rows] = ccu
    cw = cw_ref[...]
    conv = (ccu_scr[PAD - 1: PAD - 1 + rows] * cw[0:1] + ccu * cw[1:2]
            + ccu_scr[PAD + 1: PAD + 1 + rows] * cw[2:3] + cbias_ref[...])
    oc = (cb * conv).astype(BF16)

    y = (jnp.dot(og, wout_ref[0:GQA_WIDTH], preferred_element_type=F32)
         + jnp.dot(oc, wout_ref[GQA_WIDTH: GQA_WIDTH + CONV_WIDTH], preferred_element_type=F32)
         + jnp.dot(od, wout_ref[GQA_WIDTH + CONV_WIDTH:], preferred_element_type=F32))
    xmid = xe + g1 * y

    inv = lax.rsqrt(jnp.mean(xmid * xmid, axis=-1, keepdims=True) + NORM_EPS)
    h2 = ((xmid * inv) * n2g_ref[...] * (1.0 + sc2) + sh2).astype(BF16)
    a = jnp.dot(h2, up_ref[:, :D_FF], preferred_element_type=F32)
    if halo:
        a = jnp.where(valid, a, 0.0)
    a_scr[0:PAD] = jnp.zeros((PAD, D_FF), F32)
    a_scr[PAD + rows: 2 * PAD + rows] = jnp.zeros((PAD, D_FF), F32)
    a_scr[PAD: PAD + rows] = a
    base = PAD + h0
    fcw = fcw_ref[...]
    ac = (a_scr[base - 1: base - 1 + tm] * fcw[0:1] + a_scr[base: base + tm] * fcw[1:2]
          + a_scr[base + 1: base + 1 + tm] * fcw[2:3] + fcb_ref[...])
    u = jnp.dot(h2[h0: h0 + tm], up_ref[:, D_FF:], preferred_element_type=F32)
    f = (ac * jax.nn.sigmoid(ac) * u).astype(BF16)
    out_ref[0] = xmid[h0: h0 + tm] + g2 * jnp.dot(f, down_ref[...], preferred_element_type=F32)


def _post_call(x, og, od, cb, ccu, layer, mods, mod_row, cw, cbias, w_out, n2g, up, fcw, fcb, down, tm):
    b, t, d = x.shape
    nt = t // tm
    halo = nt > 1
    per = tm // HALO
    last = t // HALO - 1

    def specs(width):
        main = pl.BlockSpec((1, tm, width), lambda i, j: (i, j, 0))
        if not halo:
            return [main]
        prev = pl.BlockSpec((1, HALO, width), lambda i, j: (i, jnp.maximum(j * per - 1, 0), 0))
        nxt = pl.BlockSpec((1, HALO, width), lambda i, j: (i, jnp.minimum((j + 1) * per, last), 0))
        return [prev, main, nxt]

    in_specs, args = [], []
    for arr in (x, og, od, cb, ccu):
        sp = specs(arr.shape[-1])
        in_specs += sp
        args += [arr] * len(sp)
    in_specs += [
        _mods_spec(layer, mod_row),
        _layer_spec((3, CONV_WIDTH), layer), _layer_spec((1, CONV_WIDTH), layer),
        _layer_spec((d, d), layer), _layer_spec((1, d), layer),
        _layer_spec((d, 2 * D_FF), layer), _layer_spec((3, D_FF), layer), _layer_spec((1, D_FF), layer),
        _layer_spec((D_FF, d), layer),
    ]
    args += [mods, cw, cbias, w_out, n2g, up, fcw, fcb, down]
    rows = tm + (2 * HALO if halo else 0)
    return pl.pallas_call(
        functools.partial(_post_kernel, halo=halo, tm=tm, t_len=t),
        out_shape=jax.ShapeDtypeStruct((b, t, d), F32),
        grid=(b, nt),
        in_specs=in_specs,
        out_specs=pl.BlockSpec((1, tm, d), lambda i, j: (i, j, 0)),
        scratch_shapes=[pltpu.VMEM((rows + 2 * PAD, CONV_WIDTH), F32),
                        pltpu.VMEM((rows + 2 * PAD, D_FF), F32)],
        compiler_params=_cparams(),
        name="post_attention",
    )(*args)


def _rope_tables(n_rows, dim):
    row = jnp.repeat(jnp.arange(n_rows), GRID_W).astype(F32)
    col = jnp.tile(jnp.arange(GRID_W), n_rows).astype(F32)
    nf = dim // 4
    freqs = ROPE_THETA ** (-jnp.arange(nf, dtype=F32) / nf)
    ar = row[:, None] * freqs[None, :]
    ac = col[:, None] * freqs[None, :]
    ang = jnp.concatenate([ar, ar, ac, ac], axis=-1)
    cos, sin = jnp.cos(ang), jnp.sin(ang)
    first = ((jnp.arange(dim) // nf) % 2 == 0)[None, :]
    sin_up = jnp.where(first, -sin, 0.0)
    sin_dn = jnp.where(first, 0.0, sin)
    rep = LANES // dim
    return tuple(jnp.tile(tab, (1, rep)) for tab in (cos, sin_up, sin_dn))


def _group_matrix(group):
    idx = jnp.arange(256) // group
    return (idx[:, None] == idx[None, :]).astype(BF16)


def kernel(x_prompt, x_sample, cache_gqa_k, cache_gqa_v, cache_diff_k, cache_diff_v, c, c_ctx,
           w_mod, b_mod, norm1_g, w_in, gqa_qn_g, gqa_kn_g, conv_w, conv_b, diff_qn_g, diff_kn_g,
           diff_lambda, diff_subln_g, w_out, norm2_g, ffn_up, ffn_conv_w, ffn_conv_b, ffn_down):
    depth = w_in.shape[0]
    batch, seq, _ = x_prompt.shape
    dec_batch, dec_seq, _ = x_sample.shape

    cond_all = jnp.zeros((MOD_ROWS, D_MODEL), F32).at[0].set(c_ctx).at[1:1 + dec_batch].set(c)
    mods = _mods_call(cond_all, w_mod, b_mod)
    kkc, vogc, kdc, vodc = _cache_prep_call(cache_gqa_k, cache_gqa_v, cache_diff_k, cache_diff_v)

    w_in_b = w_in.astype(BF16)
    w_out_b = w_out.astype(BF16)
    up_b = ffn_up.astype(BF16)
    down_b = ffn_down.astype(BF16)
    g64, g32 = _group_matrix(HEAD_DIM), _group_matrix(DIFF_QK_DIM)
    rope_tabs = _rope_tables(dec_seq // GRID_W, HEAD_DIM) + _rope_tables(dec_seq // GRID_W, DIFF_QK_DIM)

    row = lambda p: p.reshape(depth, 1, -1)
    lane_tiled = lambda p, rep: jnp.tile(p, (1, rep)).reshape(depth, 1, LANES)
    gq = lane_tiled(gqa_qn_g * (HEAD_DIM ** -0.5 * LOG2E), 2)
    gk = lane_tiled(gqa_kn_g, 2)
    gqd = lane_tiled(diff_qn_g * (DIFF_QK_DIM ** -0.5 * LOG2E), 4)
    gkd = lane_tiled(diff_kn_g, 4)
    subg = lane_tiled(diff_subln_g, 2)
    qb2_g = _query_norm_bound(gq, HEAD_DIM)
    qb2_d = _query_norm_bound(gqd, DIFF_QK_DIM)
    n1g, n2g, cbias, fcb = row(norm1_g), row(norm2_g), row(conv_b), row(ffn_conv_b)

    def layer(x, l, mod_row, rope, cache_g, cache_d, own, tp, tm, tq):
        lam_init = 0.8 - 0.6 * math.exp(-0.3 * l)
        pre = _pre_call(x, l, mods, mod_row, n1g, w_in_b, gq, gk, gqd, gkd, g64, g32, rope, own, tp)
        qg, kk, vog, qd, kd, vod, cb, ccu = pre[:8]
        if cache_g is None:
            og, od = _ctx_attn_call(qg, kk, vog, qd, kd, vod, l, diff_lambda, subg, qb2_g, qb2_d, lam_init)
        else:
            og = _gqa_call(qg, kk, vog, cache_g, l, qb2_g, tq)
            od = _diff_call(qd, kd, vod, cache_d, l, diff_lambda, subg, qb2_d, lam_init, tq)
        x = _post_call(x, og, od, cb, ccu, l, mods, mod_row, conv_w, cbias, w_out_b, n2g, up_b,
                       ffn_conv_w, fcb, down_b, tm)
        return x, pre[8:]

    xp = x_prompt
    owns = []
    for l in range(depth):
        xp, own = layer(xp, l, lambda i: 0, None, None, None, True, seq, seq, seq)
        owns.append(own)
    new_gqa_k = jnp.stack([o[0] for o in owns], axis=1).reshape(batch, depth, seq, GQA_KV_HEADS, HEAD_DIM)
    new_gqa_v = jnp.stack([o[1] for o in owns], axis=1).reshape(batch, depth, seq, GQA_KV_HEADS, HEAD_DIM)
    new_diff_k = jnp.stack([o[2] for o in owns], axis=1).reshape(batch, depth, seq, DIFF_HEADS, 2, DIFF_QK_DIM)
    new_diff_v = jnp.stack([o[3] for o in owns], axis=1).reshape(batch, depth, seq, DIFF_HEADS, DIFF_V_DIM)

    xs = x_sample
    for l in range(depth):
        xs, _ = layer(xs, l, lambda i: i + 1, rope_tabs, (kkc, vogc), (kdc, vodc), False, 1024, 512, 2048)

    return (xp, xs, new_gqa_k, new_gqa_v, new_diff_k, new_diff_v)
```

```python
import functools
import math

import jax
import jax.numpy as jnp
from jax import lax
from jax.experimental import pallas as pl
from jax.experimental.pallas import tpu as pltpu

F32 = jnp.float32
BF16 = jnp.bfloat16

D_MODEL = 1024
GRID_W = 64
HEAD_DIM = 64
GQA_HEADS = 8
GQA_KV_HEADS = 2
GQA_WIDTH = GQA_HEADS * HEAD_DIM
CONV_WIDTH = 256
DIFF_HEADS = 4
DIFF_QK_DIM = 32
DIFF_V_DIM = 64
DIFF_WIDTH = DIFF_HEADS * DIFF_V_DIM
IN_WIDTH = 2304
D_FF = 2816
ROPE_THETA = 10000.0
NORM_EPS = 1e-6
N_MOD = 6
LANES = 128
MOD_ROWS = 16

OFF_QG, OFF_KG, OFF_VG = 0, 512, 640
OFF_CB, OFF_CC, OFF_CU = 768, 1024, 1280
OFF_QD, OFF_KD, OFF_VD = 1536, 1792, 2048

HALO = 16
PAD = 8

VMEM_LIMIT = 60 * 1024 * 1024

KEY_TILE = 256
SAFE_EXPONENT = 60.0
LOG2E = math.log2(math.e)
PRE_ROWS = 128
QUERY_BLOCK = 64
LATENT_PRE_ROWS = 1024
LATENT_POST_ROWS = 512
LATENT_QUERY_ROWS = 1024
EXACT_ROWS = 512


def _cparams():
    return pltpu.CompilerParams(vmem_limit_bytes=VMEM_LIMIT)


def _const_spec(shape):
    nd = len(shape)
    return pl.BlockSpec(shape, lambda *_: (0,) * nd, pipeline_mode=pl.Buffered(1))


def _layer_spec(shape, layer):
    nd = len(shape)
    return pl.BlockSpec((None,) + tuple(shape), lambda *_: (layer,) + (0,) * nd, pipeline_mode=pl.Buffered(1))


def _mods_spec(layer, mod_row):
    return pl.BlockSpec((None, 1, N_MOD, D_MODEL), lambda i, j: (layer, mod_row(i), 0, 0))


def _mods_kernel(cond_ref, w_ref, b_ref, o_ref):
    cnd = cond_ref[...]
    act = cnd * jax.nn.sigmoid(cnd)
    o_ref[0] = jnp.dot(act, w_ref[0], preferred_element_type=F32,
                       precision=lax.Precision.HIGHEST) + b_ref[0]


def _mods_call(cond_all, w_mod, b_mod):
    depth = w_mod.shape[0]
    width = N_MOD * D_MODEL
    tn = 1024
    out = pl.pallas_call(
        _mods_kernel,
        out_shape=jax.ShapeDtypeStruct((depth, MOD_ROWS, width), F32),
        grid=(depth, width // tn),
        in_specs=[
            pl.BlockSpec((MOD_ROWS, D_MODEL), lambda l, j: (0, 0)),
            pl.BlockSpec((1, D_MODEL, tn), lambda l, j: (l, 0, j)),
            pl.BlockSpec((1, 1, tn), lambda l, j: (l, 0, j)),
        ],
        out_specs=pl.BlockSpec((1, MOD_ROWS, tn), lambda l, j: (l, 0, j)),
        compiler_params=_cparams(),
        name="adaln_mods",
    )(cond_all, w_mod, b_mod.reshape(depth, 1, width))
    return out.reshape(depth, MOD_ROWS, N_MOD, D_MODEL)


def _dup_halves(x128):
    lo = lax.broadcasted_iota(jnp.int32, x128.shape, 1) < HEAD_DIM
    swapped = pltpu.roll(x128, HEAD_DIM, 1)
    return jnp.where(lo, x128, swapped), jnp.where(lo, swapped, x128)


def _cache_prep_kernel(gk_ref, gv_ref, dk_ref, dv_ref, kk_ref, vog_ref, kd_ref, vod_ref):
    ones = jnp.ones((gk_ref.shape[2], LANES), BF16)
    k0, k1 = _dup_halves(gk_ref[0, 0])
    kk_ref[0, 0, 0] = k0.astype(BF16)
    kk_ref[0, 0, 1] = k1.astype(BF16)
    v0, v1 = _dup_halves(gv_ref[0, 0])
    vog_ref[0, 0, 0, :, :LANES] = v0.astype(BF16)
    vog_ref[0, 0, 1, :, :LANES] = v1.astype(BF16)
    vog_ref[0, 0, 0, :, LANES:] = ones
    vog_ref[0, 0, 1, :, LANES:] = ones
    kd_ref[0, 0] = dk_ref[0, 0].astype(BF16)
    dv = dv_ref[0, 0]
    vod_ref[0, 0, 0, :, :LANES] = dv[:, :LANES].astype(BF16)
    vod_ref[0, 0, 1, :, :LANES] = dv[:, LANES:].astype(BF16)
    vod_ref[0, 0, 0, :, LANES:] = ones
    vod_ref[0, 0, 1, :, LANES:] = ones


def _cache_prep_call(cache_gqa_k, cache_gqa_v, cache_diff_k, cache_diff_v):
    b, depth, s = cache_gqa_k.shape[:3]
    gk = cache_gqa_k.reshape(b, depth, s, LANES)
    gv = cache_gqa_v.reshape(b, depth, s, LANES)
    dk = cache_diff_k.reshape(b, depth, s, 2 * LANES)
    dv = cache_diff_v.reshape(b, depth, s, 2 * LANES)
    idx4 = lambda i, l: (i, l, 0, 0)
    idx5 = lambda i, l: (i, l, 0, 0, 0)
    return pl.pallas_call(
        _cache_prep_kernel,
        out_shape=(
            jax.ShapeDtypeStruct((b, depth, 2, s, LANES), BF16),
            jax.ShapeDtypeStruct((b, depth, 2, s, 2 * LANES), BF16),
            jax.ShapeDtypeStruct((b, depth, s, 2 * LANES), BF16),
            jax.ShapeDtypeStruct((b, depth, 2, s, 2 * LANES), BF16),
        ),
        grid=(b, depth),
        in_specs=[
            pl.BlockSpec((1, 1, s, LANES), idx4),
            pl.BlockSpec((1, 1, s, LANES), idx4),
            pl.BlockSpec((1, 1, s, 2 * LANES), idx4),
            pl.BlockSpec((1, 1, s, 2 * LANES), idx4),
        ],
        out_specs=(
            pl.BlockSpec((1, 1, 2, s, LANES), idx5),
            pl.BlockSpec((1, 1, 2, s, 2 * LANES), idx5),
            pl.BlockSpec((1, 1, s, 2 * LANES), idx4),
            pl.BlockSpec((1, 1, 2, s, 2 * LANES), idx5),
        ),
        compiler_params=_cparams(),
        name="cache_prep",
    )(gk, gv, dk, dv)


def _group_inv_rms(z, g_ref, group):
    width = z.shape[1]
    ssum = jnp.dot((z * z).astype(BF16), g_ref[:width, :width], preferred_element_type=F32)
    return lax.rsqrt(ssum * (1.0 / group) + NORM_EPS)


def _rope_chunk(y, cos, sin_up, sin_dn, quarter):
    up = pltpu.roll(y, LANES - quarter, 1)
    dn = pltpu.roll(y, quarter, 1)
    return y * cos + up * sin_up + dn * sin_dn


def _pre_kernel(*refs, rope, own):
    it = iter(refs)
    x_ref, mod_ref, n1g_ref, w_in_ref = next(it), next(it), next(it), next(it)
    gq_ref, gk_ref, gqd_ref, gkd_ref = next(it), next(it), next(it), next(it)
    g64_ref, g32_ref = next(it), next(it)
    if rope:
        c64, su64, sd64, c32, su32, sd32 = (next(it) for _ in range(6))
    qg_ref, kk_ref, vog_ref, qd_ref, kd_ref, vod_ref, cb_ref, ccu_ref = (next(it) for _ in range(8))
    if own:
        kown_ref, vown_ref, kdown_ref, vdown_ref = (next(it) for _ in range(4))

    mods = mod_ref[0]
    sh1, sc1 = mods[0:1], mods[1:2]
    gain1 = n1g_ref[...] * (1.0 + sc1)
    ones = jnp.ones((PRE_ROWS, LANES), BF16)

    for r in range(x_ref.shape[1] // PRE_ROWS):
        rs = slice(r * PRE_ROWS, (r + 1) * PRE_ROWS)
        x = x_ref[0, rs]
        inv = lax.rsqrt(jnp.mean(x * x, axis=-1, keepdims=True) + NORM_EPS)
        h = (x * inv) * gain1 + sh1
        z = jnp.dot(h.astype(BF16), w_in_ref[...], preferred_element_type=F32)

        def rope64(y):
            if not rope:
                return y
            return _rope_chunk(y, c64[rs, :], su64[rs, :], sd64[rs, :], HEAD_DIM // 4)

        def rope32(y):
            if not rope:
                return y
            return _rope_chunk(y, c32[rs, :], su32[rs, :], sd32[rs, :], DIFF_QK_DIM // 4)

        for c in range(2):
            zc = z[:, OFF_QG + 256 * c: OFF_QG + 256 * (c + 1)]
            y = zc * _group_inv_rms(zc, g64_ref, HEAD_DIM)
            for j in range(2):
                yj = rope64(y[:, LANES * j: LANES * (j + 1)] * gq_ref[...])
                qg_ref[0, rs, 256 * c + LANES * j: 256 * c + LANES * (j + 1)] = yj.astype(BF16)

        zk = z[:, OFF_KG: OFF_KG + LANES]
        zv = z[:, OFF_VG: OFF_VG + LANES]
        kn = zk * _group_inv_rms(zk, g64_ref, HEAD_DIM) * gk_ref[...]
        if own:
            kown_ref[0, rs] = kn
            vown_ref[0, rs] = zv
        k0, k1 = _dup_halves(rope64(kn))
        kk_ref[0, 0, rs] = k0.astype(BF16)
        kk_ref[0, 1, rs] = k1.astype(BF16)
        v0, v1 = _dup_halves(zv)
        vog_ref[0, 0, rs, :LANES] = v0.astype(BF16)
        vog_ref[0, 1, rs, :LANES] = v1.astype(BF16)
        vog_ref[0, 0, rs, LANES:] = ones
        vog_ref[0, 1, rs, LANES:] = ones

        cb_ref[0, rs] = z[:, OFF_CB: OFF_CB + CONV_WIDTH]
        ccu_ref[0, rs] = z[:, OFF_CC: OFF_CC + CONV_WIDTH] * z[:, OFF_CU: OFF_CU + CONV_WIDTH]

        zq = z[:, OFF_QD: OFF_QD + 256]
        yq = zq * _group_inv_rms(zq, g32_ref, DIFF_QK_DIM)
        zkd = z[:, OFF_KD: OFF_KD + 256]
        ykd = zkd * _group_inv_rms(zkd, g32_ref, DIFF_QK_DIM)
        zvd = z[:, OFF_VD: OFF_VD + 256]
        for j in range(2):
            sl = slice(LANES * j, LANES * (j + 1))
            qd_ref[0, rs, sl] = rope32(yq[:, sl] * gqd_ref[...]).astype(BF16)
            kdj = ykd[:, sl] * gkd_ref[...]
            if own:
                kdown_ref[0, rs, sl] = kdj
            kd_ref[0, rs, sl] = rope32(kdj).astype(BF16)
            vod_ref[0, j, rs, :LANES] = zvd[:, sl].astype(BF16)
            vod_ref[0, j, rs, LANES:] = ones
        if own:
            vdown_ref[0, rs] = zvd


def _pre_call(x, layer, mods, mod_row, n1g, w_in, gq, gk, gqd, gkd, g64, g32, rope_tabs, own, tm):
    b, t, d = x.shape
    nt = t // tm
    rope = rope_tabs is not None
    gain_spec = _layer_spec((1, LANES), layer)
    in_specs = [
        pl.BlockSpec((1, tm, d), lambda i, j: (i, j, 0)),
        _mods_spec(layer, mod_row),
        _layer_spec((1, d), layer),
        _layer_spec((d, IN_WIDTH), layer),
        gain_spec, gain_spec, gain_spec, gain_spec,
        _const_spec((256, 256)), _const_spec((256, 256)),
    ]
    args = [x, mods, n1g, w_in, gq, gk, gqd, gkd, g64, g32]
    if rope:
        in_specs += [pl.BlockSpec((tm, LANES), lambda i, j: (j, 0))] * 6
        args += list(rope_tabs)
    tok = lambda w, dt: jax.ShapeDtypeStruct((b, t, w), dt)
    tok_spec = lambda w: pl.BlockSpec((1, tm, w), lambda i, j: (i, j, 0))
    pair = lambda w: jax.ShapeDtypeStruct((b, 2, t, w), BF16)
    pair_spec = lambda w: pl.BlockSpec((1, 2, tm, w), lambda i, j: (i, 0, j, 0))
    out_shape = [tok(GQA_WIDTH, BF16), pair(LANES), pair(2 * LANES), tok(256, BF16), tok(256, BF16),
                 pair(2 * LANES), tok(CONV_WIDTH, F32), tok(CONV_WIDTH, F32)]
    out_specs = [tok_spec(GQA_WIDTH), pair_spec(LANES), pair_spec(2 * LANES), tok_spec(256), tok_spec(256),
                 pair_spec(2 * LANES), tok_spec(CONV_WIDTH), tok_spec(CONV_WIDTH)]
    if own:
        out_shape += [tok(LANES, F32), tok(LANES, F32), tok(256, F32), tok(256, F32)]
        out_specs += [tok_spec(LANES), tok_spec(LANES), tok_spec(256), tok_spec(256)]
    return pl.pallas_call(
        functools.partial(_pre_kernel, rope=rope, own=own),
        out_shape=tuple(out_shape),
        grid=(b, nt),
        in_specs=in_specs,
        out_specs=tuple(out_specs),
        compiler_params=_cparams(),
        name="pre_attention",
    )(*args)


def _key_tiles(ref, lead, n_rows, lanes=slice(None)):
    def load(t):
        return ref[lead + (pl.ds(t * KEY_TILE, KEY_TILE), lanes)]
    return [functools.partial(load, t) for t in range(n_rows // KEY_TILE)]


def _row_norm2_max(x):
    xf = x.astype(F32)
    return jnp.max(jnp.sum(xf * xf, axis=-1, keepdims=True))


def _dot_nt(a, b):
    return lax.dot_general(a, b, (((1,), (1,)), ((), ())), preferred_element_type=F32)


def _scores_are_safe(k_tiles, qb2):
    kmax2 = functools.reduce(jnp.maximum, [_row_norm2_max(k()) for k in k_tiles])
    return qb2 * kmax2 <= SAFE_EXPONENT ** 2


def _softmax_pv_shifted(lhs_ref, k_tiles, v_tiles, r_ref):
    def row_block(rb, carry):
        rows = pl.ds(pl.multiple_of(rb * EXACT_ROWS, EXACT_ROWS), EXACT_ROWS)
        lhs_b = lhs_ref[rows, :]
        scores = [_dot_nt(lhs_b, k()) for k in k_tiles]
        m = functools.reduce(jnp.maximum, [jnp.max(s, axis=-1, keepdims=True) for s in scores])
        acc = None
        for s, v in zip(scores, v_tiles):
            p = jnp.exp2(s - m).astype(BF16)
            part = jnp.dot(p, v(), preferred_element_type=F32)
            acc = part if acc is None else acc + part
        r_ref[rows, :] = acc[:, :LANES] / acc[:, LANES:]
        return carry

    lax.fori_loop(0, lhs_ref.shape[0] // EXACT_ROWS, row_block, 0)


def _group_unshifted(tq, group):
    lhs_block, finish, k_tiles, v_tiles = group
    for qb in range(tq // QUERY_BLOCK):
        lhs_b = lhs_block(qb)
        acc = None
        for k, v in zip(k_tiles, v_tiles):
            p = jnp.exp2(_dot_nt(lhs_b, k())).astype(BF16)
            part = jnp.dot(p, v(), preferred_element_type=F32)
            acc = part if acc is None else acc + part
        finish(acc[:, :LANES] / acc[:, LANES:], qb)


def _group_shifted(tq, group, r_ref, lhs_ref):
    lhs_block, finish, k_tiles, v_tiles = group
    rows = 4 * QUERY_BLOCK
    for qb in range(tq // QUERY_BLOCK):
        lhs_ref[qb * rows:(qb + 1) * rows, :] = lhs_block(qb)
    _softmax_pv_shifted(lhs_ref, k_tiles, v_tiles, r_ref)
    for qb in range(tq // QUERY_BLOCK):
        finish(r_ref[qb * rows:(qb + 1) * rows, :], qb)


def _attention(tq, groups, safes, r_ref, lhs_ref):
    all_safe = functools.reduce(jnp.logical_and, safes)

    @pl.when(all_safe)
    def _():
        for group in groups:
            _group_unshifted(tq, group)

    @pl.when(jnp.logical_not(all_safe))
    def _():
        for group, safe in zip(groups, safes):
            if len(groups) > 1:
                pl.when(safe)(functools.partial(_group_unshifted, tq, group))
            pl.when(jnp.logical_not(safe))(functools.partial(_group_shifted, tq, group, r_ref, lhs_ref))


def _gqa_group(q_ref, o_ref, lane0, k_tiles, v_tiles):
    nq = QUERY_BLOCK
    lo = lax.broadcasted_iota(jnp.int32, (nq, LANES), 1) < HEAD_DIM
    zero = jnp.zeros((nq, LANES), BF16)

    def lhs_block(qb):
        rows = slice(qb * nq, (qb + 1) * nq)
        qa = q_ref[0, rows, lane0:lane0 + LANES]
        qc = q_ref[0, rows, lane0 + LANES:lane0 + 2 * LANES]
        return jnp.concatenate([jnp.where(lo, qa, zero), jnp.where(lo, zero, qa),
                                jnp.where(lo, qc, zero), jnp.where(lo, zero, qc)], axis=0)

    def finish(r, qb):
        rows = slice(qb * nq, (qb + 1) * nq)
        o_ref[0, rows, lane0:lane0 + LANES] = jnp.where(lo, r[0:nq], r[nq:2 * nq]).astype(BF16)
        o_ref[0, rows, lane0 + LANES:lane0 + 2 * LANES] = (
            jnp.where(lo, r[2 * nq:3 * nq], r[3 * nq:4 * nq]).astype(BF16))

    return lhs_block, finish, k_tiles, v_tiles


def _diff_lambda(lam_ref, lam_init):
    lf = lam_ref[...]
    return (jnp.exp(jnp.sum(lf[0:1] * lf[1:2], axis=-1, keepdims=True))
            - jnp.exp(jnp.sum(lf[2:3] * lf[3:4], axis=-1, keepdims=True)) + lam_init)


def _diff_group(q_ref, o_ref, lane0, k_tiles, v_tiles, lam, gain):
    nq = QUERY_BLOCK
    lane = lax.broadcasted_iota(jnp.int32, (nq, LANES), 1)
    lo = lane < DIFF_V_DIM
    zero = jnp.zeros((nq, LANES), BF16)

    def lhs_block(qb):
        q = q_ref[0, qb * nq:(qb + 1) * nq, lane0:lane0 + LANES]
        return jnp.concatenate(
            [jnp.where((lane >= DIFF_QK_DIM * u) & (lane < DIFF_QK_DIM * (u + 1)), q, zero) for u in range(4)],
            axis=0)

    def finish(r, qb):
        o = jnp.where(lo, r[0:nq] - lam * r[nq:2 * nq], r[2 * nq:3 * nq] - lam * r[3 * nq:4 * nq])
        sq = o * o
        ms_lo = jnp.sum(jnp.where(lo, sq, 0.0), axis=-1, keepdims=True) * (1.0 / DIFF_V_DIM)
        ms_hi = jnp.sum(jnp.where(lo, 0.0, sq), axis=-1, keepdims=True) * (1.0 / DIFF_V_DIM)
        inv = jnp.where(lo, lax.rsqrt(ms_lo + NORM_EPS), lax.rsqrt(ms_hi + NORM_EPS))
        o_ref[0, qb * nq:(qb + 1) * nq, lane0:lane0 + LANES] = ((o * inv) * gain).astype(BF16)

    return lhs_block, finish, k_tiles, v_tiles


def _query_norm_bound(gain, dim):
    return dim * 1.02 * jnp.max(gain * gain, axis=(1, 2))


_SMEM_SPEC = pl.BlockSpec(memory_space=pltpu.SMEM)


def _attn_scratch(tq):
    return [pltpu.VMEM((4 * tq, LANES), F32), pltpu.VMEM((4 * tq, LANES), BF16)]


def _kept_safe_flag(k_tiles, qb2, safe_ref):
    @pl.when(pl.program_id(2) == 0)
    def _():
        safe_ref[0] = _scores_are_safe(k_tiles, qb2).astype(jnp.int32)
    return safe_ref[0] == 1


def _latent_cparams():
    return pltpu.CompilerParams(vmem_limit_bytes=VMEM_LIMIT,
                                dimension_semantics=("arbitrary", "arbitrary", "arbitrary"))


def _gqa_kernel(q_ref, kk_ref, vo_ref, kkc_ref, voc_ref, qb2_ref, o_ref, r_ref, lhs_ref, safe_ref, *, layer):
    k_tiles = _key_tiles(kk_ref, (0, 0), kk_ref.shape[2]) + _key_tiles(kkc_ref, (0, 0, 0), kkc_ref.shape[3])
    v_tiles = _key_tiles(vo_ref, (0, 0), vo_ref.shape[2]) + _key_tiles(voc_ref, (0, 0, 0), voc_ref.shape[3])
    safe = _kept_safe_flag(k_tiles, qb2_ref[layer], safe_ref)
    _attention(q_ref.shape[1], [_gqa_group(q_ref, o_ref, 0, k_tiles, v_tiles)], [safe], r_ref, lhs_ref)


def _gqa_call(qg, kk, vog, cache, layer, qb2, tq):
    b, t, _ = qg.shape
    kkc, voc = cache
    s = kkc.shape[3]
    return pl.pallas_call(
        functools.partial(_gqa_kernel, layer=layer),
        out_shape=jax.ShapeDtypeStruct((b, t, GQA_WIDTH), BF16),
        grid=(b, GQA_KV_HEADS, t // tq),
        in_specs=[
            pl.BlockSpec((1, tq, 256), lambda i, h, j: (i, j, h)),
            pl.BlockSpec((1, 1, t, LANES), lambda i, h, j: (i, h, 0, 0)),
            pl.BlockSpec((1, 1, t, 2 * LANES), lambda i, h, j: (i, h, 0, 0)),
            pl.BlockSpec((1, 1, 1, s, LANES), lambda i, h, j: (i, layer, h, 0, 0)),
            pl.BlockSpec((1, 1, 1, s, 2 * LANES), lambda i, h, j: (i, layer, h, 0, 0)),
            _SMEM_SPEC,
        ],
        out_specs=pl.BlockSpec((1, tq, 256), lambda i, h, j: (i, j, h)),
        scratch_shapes=_attn_scratch(tq) + [pltpu.SMEM((1,), jnp.int32)],
        compiler_params=_latent_cparams(),
        name="gqa_attention",
    )(qg, kk, vog, kkc, voc, qb2)


def _diff_kernel(q_ref, k_ref, vo_ref, kc_ref, voc_ref, lam_ref, sg_ref, qb2_ref, o_ref, r_ref, lhs_ref, safe_ref,
                 *, layer, lam_init):
    k_tiles = _key_tiles(k_ref, (0,), k_ref.shape[1]) + _key_tiles(kc_ref, (0, 0), kc_ref.shape[2])
    v_tiles = _key_tiles(vo_ref, (0, 0), vo_ref.shape[2]) + _key_tiles(voc_ref, (0, 0, 0), voc_ref.shape[3])
    safe = _kept_safe_flag(k_tiles, qb2_ref[layer], safe_ref)
    group = _diff_group(q_ref, o_ref, 0, k_tiles, v_tiles, _diff_lambda(lam_ref, lam_init),
                        sg_ref[...] * (1.0 - lam_init))
    _attention(q_ref.shape[1], [group], [safe], r_ref, lhs_ref)


def _diff_call(qd, kd, vod, cache, layer, lam_p, subg, qb2, lam_init, tq):
    b, t, _ = qd.shape
    kdc, vodc = cache
    s = kdc.shape[2]
    return pl.pallas_call(
        functools.partial(_diff_kernel, layer=layer, lam_init=lam_init),
        out_shape=jax.ShapeDtypeStruct((b, t, DIFF_WIDTH), BF16),
        grid=(b, 2, t // tq),
        in_specs=[
            pl.BlockSpec((1, tq, LANES), lambda i, h, j: (i, j, h)),
            pl.BlockSpec((1, t, LANES), lambda i, h, j: (i, 0, h)),
            pl.BlockSpec((1, 1, t, 2 * LANES), lambda i, h, j: (i, h, 0, 0)),
            pl.BlockSpec((1, 1, s, LANES), lambda i, h, j: (i, layer, 0, h)),
            pl.BlockSpec((1, 1, 1, s, 2 * LANES), lambda i, h, j: (i, layer, h, 0, 0)),
            _layer_spec((4, DIFF_QK_DIM), layer), _layer_spec((1, LANES), layer), _SMEM_SPEC,
        ],
        out_specs=pl.BlockSpec((1, tq, LANES), lambda i, h, j: (i, j, h)),
        scratch_shapes=_attn_scratch(tq) + [pltpu.SMEM((1,), jnp.int32)],
        compiler_params=_latent_cparams(),
        name="diff_attention",
    )(qd, kd, vod, kdc, vodc, lam_p, subg, qb2)


def _ctx_attn_kernel(qg_ref, kk_ref, vog_ref, qd_ref, kd_ref, vod_ref, lam_ref, sg_ref, qb2g_ref, qb2d_ref,
                     og_ref, od_ref, r_ref, lhs_ref, *, layer, lam_init):
    t = qg_ref.shape[1]
    lam = _diff_lambda(lam_ref, lam_init)
    gain = sg_ref[...] * (1.0 - lam_init)
    groups, safes = [], []
    for kvh in range(GQA_KV_HEADS):
        k_tiles, v_tiles = _key_tiles(kk_ref, (0, kvh), t), _key_tiles(vog_ref, (0, kvh), t)
        groups.append(_gqa_group(qg_ref, og_ref, 2 * LANES * kvh, k_tiles, v_tiles))
        safes.append(_scores_are_safe(k_tiles, qb2g_ref[layer]))
    for pair in range(DIFF_HEADS // 2):
        lanes = slice(LANES * pair, LANES * (pair + 1))
        k_tiles, v_tiles = _key_tiles(kd_ref, (0,), t, lanes), _key_tiles(vod_ref, (0, pair), t)
        groups.append(_diff_group(qd_ref, od_ref, LANES * pair, k_tiles, v_tiles, lam, gain))
        safes.append(_scores_are_safe(k_tiles, qb2d_ref[layer]))
    _attention(t, groups, safes, r_ref, lhs_ref)


def _ctx_attn_call(qg, kk, vog, qd, kd, vod, layer, lam_p, subg, qb2_g, qb2_d, lam_init):
    b, t, _ = qg.shape
    tok = lambda w: pl.BlockSpec((1, t, w), lambda i: (i, 0, 0))
    pair = lambda w: pl.BlockSpec((1, 2, t, w), lambda i: (i, 0, 0, 0))
    return pl.pallas_call(
        functools.partial(_ctx_attn_kernel, layer=layer, lam_init=lam_init),
        out_shape=(jax.ShapeDtypeStruct((b, t, GQA_WIDTH), BF16), jax.ShapeDtypeStruct((b, t, DIFF_WIDTH), BF16)),
        grid=(b,),
        in_specs=[tok(GQA_WIDTH), pair(LANES), pair(2 * LANES), tok(DIFF_WIDTH), tok(DIFF_WIDTH), pair(2 * LANES),
                  _layer_spec((4, DIFF_QK_DIM), layer), _layer_spec((1, LANES), layer), _SMEM_SPEC, _SMEM_SPEC],
        out_specs=(tok(GQA_WIDTH), tok(DIFF_WIDTH)),
        scratch_shapes=_attn_scratch(t),
        compiler_params=_cparams(),
        name="context_attention",
    )(qg, kk, vog, qd, kd, vod, lam_p, subg, qb2_g, qb2_d)


def _post_kernel(*refs, halo, tm, t_len):
    it = iter(refs)
    n_blk = 3 if halo else 1

    def take():
        blocks = [next(it) for _ in range(n_blk)]
        if halo:
            return jnp.concatenate([blocks[0][0], blocks[1][0], blocks[2][0]], axis=0)
        return blocks[0][0]

    xe, og, od, cb, ccu = take(), take(), take(), take(), take()
    mod_ref, cw_ref, cbias_ref, wout_ref, n2g_ref = (next(it) for _ in range(5))
    up_ref, fcw_ref, fcb_ref, down_ref = (next(it) for _ in range(4))
    out_ref, ccu_scr, a_scr = next(it), next(it), next(it)

    h0 = HALO if halo else 0
    rows = tm + 2 * h0
    mods = mod_ref[0]
    g1, sh2, sc2, g2 = mods[2:3], mods[3:4], mods[4:5], mods[5:6]

    if halo:
        pos = pl.program_id(1) * tm - h0 + lax.broadcasted_iota(jnp.int32, (rows, 1), 0)
        valid = (pos >= 0) & (pos < t_len)
        ccu = jnp.where(valid, ccu, 0.0)

    ccu_scr[0:PAD] = jnp.zeros((PAD, CONV_WIDTH), F32)
    ccu_scr[PAD + rows: 2 * PAD + rows] = jnp.zeros((PAD, CONV_WIDTH), F32)
    ccu_scr[PAD: PAD + rows] = ccu
    cw = cw_ref[...]
    conv = (ccu_scr[PAD - 1: PAD - 1 + rows] * cw[0:1] + ccu * cw[1:2]
            + ccu_scr[PAD + 1: PAD + 1 + rows] * cw[2:3] + cbias_ref[...])
    oc = (cb * conv).astype(BF16)

    y = (jnp.dot(og, wout_ref[0:GQA_WIDTH], preferred_element_type=F32)
         + jnp.dot(oc, wout_ref[GQA_WIDTH: GQA_WIDTH + CONV_WIDTH], preferred_element_type=F32)
         + jnp.dot(od, wout_ref[GQA_WIDTH + CONV_WIDTH:], preferred_element_type=F32))
    xmid = xe + g1 * y

    inv = lax.rsqrt(jnp.mean(xmid * xmid, axis=-1, keepdims=True) + NORM_EPS)
    h2 = ((xmid * inv) * n2g_ref[...] * (1.0 + sc2) + sh2).astype(BF16)
    a = jnp.dot(h2, up_ref[:, :D_FF], preferred_element_type=F32)
    if halo:
        a = jnp.where(valid, a, 0.0)
    a_scr[0:PAD] = jnp.zeros((PAD, D_FF), F32)
    a_scr[PAD + rows: 2 * PAD + rows] = jnp.zeros((PAD, D_FF), F32)
    a_scr[PAD: PAD + rows] = a
    base = PAD + h0
    fcw = fcw_ref[...]
    ac = (a_scr[base - 1: base - 1 + tm] * fcw[0:1] + a_scr[base: base + tm] * fcw[1:2]
          + a_scr[base + 1: base + 1 + tm] * fcw[2:3] + fcb_ref[...])
    u = jnp.dot(h2[h0: h0 + tm], up_ref[:, D_FF:], preferred_element_type=F32)
    f = (ac * jax.nn.sigmoid(ac) * u).astype(BF16)
    out_ref[0] = xmid[h0: h0 + tm] + g2 * jnp.dot(f, down_ref[...], preferred_element_type=F32)


def _post_call(x, og, od, cb, ccu, layer, mods, mod_row, cw, cbias, w_out, n2g, up, fcw, fcb, down, tm):
    b, t, d = x.shape
    nt = t // tm
    halo = nt > 1
    per = tm // HALO
    last = t // HALO - 1

    def specs(width):
        main = pl.BlockSpec((1, tm, width), lambda i, j: (i, j, 0))
        if not halo:
            return [main]
        prev = pl.BlockSpec((1, HALO, width), lambda i, j: (i, jnp.maximum(j * per - 1, 0), 0))
        nxt = pl.BlockSpec((1, HALO, width), lambda i, j: (i, jnp.minimum((j + 1) * per, last), 0))
        return [prev, main, nxt]

    in_specs, args = [], []
    for arr in (x, og, od, cb, ccu):
        sp = specs(arr.shape[-1])
        in_specs += sp
        args += [arr] * len(sp)
    in_specs += [
        _mods_spec(layer, mod_row),
        _layer_spec((3, CONV_WIDTH), layer), _layer_spec((1, CONV_WIDTH), layer),
        _layer_spec((d, d), layer), _layer_spec((1, d), layer),
        _layer_spec((d, 2 * D_FF), layer), _layer_spec((3, D_FF), layer), _layer_spec((1, D_FF), layer),
        _layer_spec((D_FF, d), layer),
    ]
    args += [mods, cw, cbias, w_out, n2g, up, fcw, fcb, down]
    rows = tm + (2 * HALO if halo else 0)
    return pl.pallas_call(
        functools.partial(_post_kernel, halo=halo, tm=tm, t_len=t),
        out_shape=jax.ShapeDtypeStruct((b, t, d), F32),
        grid=(b, nt),
        in_specs=in_specs,
        out_specs=pl.BlockSpec((1, tm, d), lambda i, j: (i, j, 0)),
        scratch_shapes=[pltpu.VMEM((rows + 2 * PAD, CONV_WIDTH), F32),
                        pltpu.VMEM((rows + 2 * PAD, D_FF), F32)],
        compiler_params=_cparams(),
        name="post_attention",
    )(*args)


def _rope_tables(n_rows, dim):
    row = jnp.repeat(jnp.arange(n_rows), GRID_W).astype(F32)
    col = jnp.tile(jnp.arange(GRID_W), n_rows).astype(F32)
    nf = dim // 4
    freqs = ROPE_THETA ** (-jnp.arange(nf, dtype=F32) / nf)
    ar = row[:, None] * freqs[None, :]
    ac = col[:, None] * freqs[None, :]
    ang = jnp.concatenate([ar, ar, ac, ac], axis=-1)
    cos, sin = jnp.cos(ang), jnp.sin(ang)
    first = ((jnp.arange(dim) // nf) % 2 == 0)[None, :]
    sin_up = jnp.where(first, -sin, 0.0)
    sin_dn = jnp.where(first, 0.0, sin)
    rep = LANES // dim
    return tuple(jnp.tile(tab, (1, rep)) for tab in (cos, sin_up, sin_dn))


def _group_matrix(group):
    idx = jnp.arange(256) // group
    return (idx[:, None] == idx[None, :]).astype(BF16)


def kernel(x_prompt, x_sample, cache_gqa_k, cache_gqa_v, cache_diff_k, cache_diff_v, c, c_ctx,
           w_mod, b_mod, norm1_g, w_in, gqa_qn_g, gqa_kn_g, conv_w, conv_b, diff_qn_g, diff_kn_g,
           diff_lambda, diff_subln_g, w_out, norm2_g, ffn_up, ffn_conv_w, ffn_conv_b, ffn_down):
    depth = w_in.shape[0]
    batch, seq, _ = x_prompt.shape
    dec_batch, dec_seq, _ = x_sample.shape

    cond_all = jnp.zeros((MOD_ROWS, D_MODEL), F32).at[0].set(c_ctx).at[1:1 + dec_batch].set(c)
    mods = _mods_call(cond_all, w_mod, b_mod)
    kkc, vogc, kdc, vodc = _cache_prep_call(cache_gqa_k, cache_gqa_v, cache_diff_k, cache_diff_v)

    w_in_b = w_in.astype(BF16)
    w_out_b = w_out.astype(BF16)
    up_b = ffn_up.astype(BF16)
    down_b = ffn_down.astype(BF16)
    g64, g32 = _group_matrix(HEAD_DIM), _group_matrix(DIFF_QK_DIM)
    rope_tabs = _rope_tables(dec_seq // GRID_W, HEAD_DIM) + _rope_tables(dec_seq // GRID_W, DIFF_QK_DIM)

    row = lambda p: p.reshape(depth, 1, -1)
    lane_tiled = lambda p, rep: jnp.tile(p, (1, rep)).reshape(depth, 1, LANES)
    gq = lane_tiled(gqa_qn_g * (HEAD_DIM ** -0.5 * LOG2E), 2)
    gk = lane_tiled(gqa_kn_g, 2)
    gqd = lane_tiled(diff_qn_g * (DIFF_QK_DIM ** -0.5 * LOG2E), 4)
    gkd = lane_tiled(diff_kn_g, 4)
    subg = lane_tiled(diff_subln_g, 2)
    qb2_g = _query_norm_bound(gq, HEAD_DIM)
    qb2_d = _query_norm_bound(gqd, DIFF_QK_DIM)
    n1g, n2g, cbias, fcb = row(norm1_g), row(norm2_g), row(conv_b), row(ffn_conv_b)

    def layer(x, l, mod_row, rope, cache_g, cache_d, own, tp, tm, tq):
        lam_init = 0.8 - 0.6 * math.exp(-0.3 * l)
        pre = _pre_call(x, l, mods, mod_row, n1g, w_in_b, gq, gk, gqd, gkd, g64, g32, rope, own, tp)
        qg, kk, vog, qd, kd, vod, cb, ccu = pre[:8]
        if cache_g is None:
            og, od = _ctx_attn_call(qg, kk, vog, qd, kd, vod, l, diff_lambda, subg, qb2_g, qb2_d, lam_init)
        else:
            og = _gqa_call(qg, kk, vog, cache_g, l, qb2_g, tq)
            od = _diff_call(qd, kd, vod, cache_d, l, diff_lambda, subg, qb2_d, lam_init, tq)
        x = _post_call(x, og, od, cb, ccu, l, mods, mod_row, conv_w, cbias, w_out_b, n2g, up_b,
                       ffn_conv_w, fcb, down_b, tm)
        return x, pre[8:]

    xp = x_prompt
    owns = []
    for l in range(depth):
        xp, own = layer(xp, l, lambda i: 0, None, None, None, True, seq, seq, seq)
        owns.append(own)
    new_gqa_k = jnp.stack([o[0] for o in owns], axis=1).reshape(batch, depth, seq, GQA_KV_HEADS, HEAD_DIM)
    new_gqa_v = jnp.stack([o[1] for o in owns], axis=1).reshape(batch, depth, seq, GQA_KV_HEADS, HEAD_DIM)
    new_diff_k = jnp.stack([o[2] for o in owns], axis=1).reshape(batch, depth, seq, DIFF_HEADS, 2, DIFF_QK_DIM)
    new_diff_v = jnp.stack([o[3] for o in owns], axis=1).reshape(batch, depth, seq, DIFF_HEADS, DIFF_V_DIM)

    xs = x_sample
    for l in range(depth):
        xs, _ = layer(xs, l, lambda i: i + 1, rope_tabs, (kkc, vogc), (kdc, vodc), False,
                      LATENT_PRE_ROWS, LATENT_POST_ROWS, LATENT_QUERY_ROWS)

    return (xp, xs, new_gqa_k, new_gqa_v, new_diff_k, new_diff_v)
```
